```python
import math
import jax, jax.numpy as jnp
from jax import lax
import numpy as np

D_MODEL = 2048
BATCH = 8
SEQ = 4096
DEPTH = 2

CHUNK = 64
HEAD_DIM = 64
D_MIX = D_MODEL
A_WIDTH = D_MIX // 4
B_WIDTH = D_MIX // 4
C_WIDTH = D_MIX // 4
D_WIDTH = D_MIX - A_WIDTH - B_WIDTH - C_WIDTH
A_HEADS = A_WIDTH // HEAD_DIM
B_HEADS = B_WIDTH // HEAD_DIM
C_HEADS = C_WIDTH // (2 * HEAD_DIM)
A_DECAY_LORA = 64
A_ICLR_LORA = 64
A_GATE_LORA = 128
B_LEFT_CHUNKS = 8
B_BAND = (B_LEFT_CHUNKS + 1) * CHUNK
REL_CLIP = 128
T5_BUCKETS = 32
T5_MAX_DIST = 128
Q_BLOCK = 128
CONV_WIDTH = 3
N_GROUPS = 4
EXPERTS_PER_GROUP = 8
N_EXPERTS = N_GROUPS * EXPERTS_PER_GROUP
TOP_K_IN_GROUP = 2
EXPERT_HIDDEN = 256
RMS_EPS = 1e-6
RWKV_GN_EPS = 64e-5
SUBLN_EPS = 1e-5
NEG_INF = -1e30

A_COLS = 3 * A_WIDTH + A_DECAY_LORA + A_ICLR_LORA + A_GATE_LORA
B_COLS = 3 * B_WIDTH
C_COLS = 3 * C_WIDTH
D_COLS = 3 * D_WIDTH
IN_COLS = A_COLS + B_COLS + C_COLS + D_COLS

kernel_name = "hybrid_parallel_heads_streaming_encoder"


def rmsnorm(x, g, eps=RMS_EPS):
    xf = x.astype(jnp.float32)
    y = xf * lax.rsqrt(jnp.mean(xf * xf, axis=-1, keepdims=True) + eps)
    return (y * g.astype(jnp.float32)).astype(x.dtype)


def shift_right(x):
    return jnp.pad(x, ((0, 0), (1, 0), (0, 0)))[:, :-1]


def rwkv7_step(state, inp):
    r, w, k, v, a, b = inp
    sa = jnp.einsum('bhvk,bhk->bhv', state, a)
    state = (state * w[:, :, None, :] + sa[..., None] * b[:, :, None, :]
             + v[..., None] * k[:, :, None, :])
    y = jnp.einsum('bhvk,bhk->bhv', state, r)
    return state, y


def rwkv7_mixer(p, mu, w0, w2, a0, a2, g2, k_k, k_a, r_k, gn_w, gn_b):
    dtype = p.dtype
    p = p.astype(jnp.float32)
    bsz, slen, _ = p.shape
    p = p + (shift_right(p) - p) * mu
    o1 = A_WIDTH; o2 = 2 * A_WIDTH; o3 = 3 * A_WIDTH
    o4 = o3 + A_DECAY_LORA; o5 = o4 + A_ICLR_LORA
    r, k, v, w_lo, a_lo, g_lo = jnp.split(p, [o1, o2, o3, o4, o5], axis=-1)
    w = -jax.nn.softplus(-(w0 + jnp.tanh(w_lo) @ w2)) - 0.5
    decay = jnp.exp(-jnp.exp(w))
    a = jax.nn.sigmoid(a0 + a_lo @ a2)
    g = jax.nn.sigmoid(g_lo) @ g2
    heads = lambda t: t.reshape(bsz, slen, A_HEADS, HEAD_DIM)
    kk = heads(k * k_k)
    kk = kk / jnp.maximum(jnp.sqrt(jnp.sum(kk * kk, -1, keepdims=True)), 1e-12)
    k = k * (1.0 + (a - 1.0) * k_a)
    rh, kh, vh, ah, dh = heads(r), heads(k), heads(v), heads(a), heads(decay)
    xs = tuple(jnp.moveaxis(t, 1, 0) for t in (rh, dh, kh, vh, -kk, kk * ah))
    state0 = jnp.zeros((bsz, A_HEADS, HEAD_DIM, HEAD_DIM), jnp.float32)
    _, y = lax.scan(rwkv7_step, state0, xs)
    y = jnp.moveaxis(y, 0, 1)
    mean = jnp.mean(y, -1, keepdims=True)
    var = jnp.mean((y - mean) ** 2, -1, keepdims=True)
    y = ((y - mean) * lax.rsqrt(var + RWKV_GN_EPS)).reshape(bsz, slen, A_WIDTH) * gn_w + gn_b
    bonus = jnp.sum(rh * kh * r_k, -1, keepdims=True) * vh
    y = y + bonus.reshape(bsz, slen, A_WIDTH)
    return (y * g).astype(dtype)


def chunk_band_attention(q, k, v, rel_bias):
    bsz, slen, nh, dh = q.shape
    n_chunks = slen // CHUNK
    pad = B_LEFT_CHUNKS * CHUNK
    k_pad = jnp.pad(k, ((0, 0), (pad, 0), (0, 0), (0, 0)))
    v_pad = jnp.pad(v, ((0, 0), (pad, 0), (0, 0), (0, 0)))
    qi = jnp.arange(CHUNK)
    kl = jnp.arange(B_BAND)
    dist = (pad + qi[:, None]) - kl[None, :]
    bias = rel_bias[:, jnp.clip(dist, -REL_CLIP, REL_CLIP) + REL_CLIP].astype(jnp.float32)
    q_chunks = jnp.moveaxis(q.reshape(bsz, n_chunks, CHUNK, nh, dh), 1, 0)
    scale = dh ** -0.5

    def one_chunk(args):
        c, qc = args
        kc = lax.dynamic_slice_in_dim(k_pad, c * CHUNK, B_BAND, axis=1)
        vc = lax.dynamic_slice_in_dim(v_pad, c * CHUNK, B_BAND, axis=1)
        s = jnp.einsum('bqhd,bkhd->bhqk', qc, kc).astype(jnp.float32) * scale + bias
        valid = (c * CHUNK - pad + kl) >= 0
        s = jnp.where(valid[None, None, None, :], s, NEG_INF)
        pr = jax.nn.softmax(s, axis=-1)
        return jnp.einsum('bhqk,bkhd->bqhd', pr.astype(vc.dtype), vc)

    out = lax.map(one_chunk, (jnp.arange(n_chunks), q_chunks))
    return jnp.moveaxis(out, 0, 1).reshape(bsz, slen, nh * dh)


def t5_buckets(rel):
    nb = T5_BUCKETS // 2
    max_exact = nb // 2
    ret = (rel > 0).astype(jnp.int32) * nb
    n = jnp.abs(rel)
    nf = jnp.maximum(n, 1).astype(jnp.float32)
    large = max_exact + (jnp.log(nf / max_exact) / math.log(T5_MAX_DIST / max_exact)
                         * (nb - max_exact)).astype(jnp.int32)
    large = jnp.minimum(large, nb - 1)
    return ret + jnp.where(n < max_exact, n, large)


def diff_attention(q, k, v, t5_table, lam, lam_init, subln_w):
    bsz, slen, nh, _, dh = q.shape
    nq = slen // Q_BLOCK
    q_blocks = jnp.moveaxis(q.reshape(bsz, nq, Q_BLOCK, nh, 2, dh), 1, 0)
    key_pos = jnp.arange(slen)
    scale = dh ** -0.5

    def one_block(args):
        bi, qb = args
        q_pos = bi * Q_BLOCK + jnp.arange(Q_BLOCK)
        bias = t5_table[t5_buckets(key_pos[None, :] - q_pos[:, None])]
        bias = jnp.moveaxis(bias, -1, 0).astype(jnp.float32)
        allowed = (key_pos[None, :] // CHUNK) <= (q_pos[:, None] // CHUNK)
        s = jnp.einsum('bqhcd,bkhcd->bhcqk', qb, k).astype(jnp.float32) * scale + bias[None, :, None]
        s = jnp.where(allowed, s, NEG_INF)
        pr = jax.nn.softmax(s, axis=-1)
        attn = pr[:, :, 0] - lam * pr[:, :, 1]
        return jnp.einsum('bhqk,bkhd->bqhd', attn.astype(v.dtype), v)

    out = lax.map(one_block, (jnp.arange(nq), q_blocks))
    out = jnp.moveaxis(out, 0, 1).reshape(bsz, slen, nh, 2 * dh).astype(jnp.float32)
    out = out * lax.rsqrt(jnp.mean(out * out, -1, keepdims=True) + SUBLN_EPS)
    out = out * subln_w * (1.0 - lam_init)
    return out.reshape(bsz, slen, nh * 2 * dh).astype(v.dtype)


def short_conv_mixer(p, conv_w):
    b_gate, c_gate, h = jnp.split(p, 3, axis=-1)
    u = c_gate * h
    slen = u.shape[1]
    u_pad = jnp.pad(u, ((0, 0), (CONV_WIDTH - 1, 0), (0, 0)))
    y = conv_w[0] * u_pad[:, :slen] + conv_w[1] * u_pad[:, 1:slen + 1] + conv_w[2] * u_pad[:, 2:slen + 2]
    return b_gate * y


def hier_moe(x, wg_r, bg_r, we_r, be_r, w_gate, w_up, w_down):
    bsz, slen, d = x.shape
    xt = x.reshape(-1, d)
    g_logits = (xt @ wg_r).astype(jnp.float32) + bg_r
    g_prob = jax.nn.softmax(g_logits, axis=-1)
    g_idx = jnp.argmax(g_logits, axis=-1)
    g_w = jnp.take_along_axis(g_prob, g_idx[:, None], axis=-1)
    e_logits = ((xt @ we_r).astype(jnp.float32) + be_r).reshape(-1, N_GROUPS, EXPERTS_PER_GROUP)
    e_sel = jnp.take_along_axis(e_logits, g_idx[:, None, None], axis=1)[:, 0]
    e_prob = jax.nn.softmax(e_sel, axis=-1)
    top_p, top_i = lax.top_k(e_prob, TOP_K_IN_GROUP)
    top_p = top_p / jnp.sum(top_p, -1, keepdims=True)
    weights = g_w * top_p
    expert_ids = g_idx[:, None] * EXPERTS_PER_GROUP + top_i
    combine = jnp.sum(jax.nn.one_hot(expert_ids, N_EXPERTS, dtype=jnp.float32) * weights[..., None], axis=1)
    out = jnp.zeros(xt.shape, jnp.float32)
    for e in range(N_EXPERTS):
        h = jax.nn.silu(xt @ w_gate[e]) * (xt @ w_up[e])
        out = out + combine[:, e:e + 1] * (h @ w_down[e]).astype(jnp.float32)
    return out.astype(x.dtype).reshape(bsz, slen, d)


def setup_inputs(seed: int = 0) -> dict:
    key = jax.random.key(seed)
    ks = jax.random.split(key, 32)
    f32 = jnp.float32
    nrm = lambda k, shape, s: jax.random.normal(k, shape, f32) * s
    L = DEPTH
    return {
        "x": nrm(ks[0], (BATCH, SEQ, D_MODEL), 1.0),
        "w_in": nrm(ks[1], (L, D_MODEL, IN_COLS), D_MODEL ** -0.5),
        "w_out": nrm(ks[2], (L, D_MIX, D_MODEL), D_MIX ** -0.5),
        "norm_mix": 1.0 + nrm(ks[3], (L, D_MODEL), 0.02),
        "norm_ffn": 1.0 + nrm(ks[4], (L, D_MODEL), 0.02),
        "norm_final": 1.0 + nrm(ks[5], (D_MODEL,), 0.02),
        "rwkv_mu": jax.random.uniform(ks[6], (L, A_COLS), f32, 0.1, 0.9),
        "rwkv_w0": nrm(ks[7], (L, A_WIDTH), 0.5) - 0.5,
        "rwkv_w2": nrm(ks[8], (L, A_DECAY_LORA, A_WIDTH), 0.5 * A_DECAY_LORA ** -0.5),
        "rwkv_a0": nrm(ks[9], (L, A_WIDTH), 0.1),
        "rwkv_a2": nrm(ks[10], (L, A_ICLR_LORA, A_WIDTH), 0.5 * A_ICLR_LORA ** -0.5),
        "rwkv_g2": nrm(ks[11], (L, A_GATE_LORA, A_WIDTH), A_GATE_LORA ** -0.5),
        "rwkv_k_k": 0.85 + nrm(ks[12], (L, A_WIDTH), 0.02),
        "rwkv_k_a": 1.0 + nrm(ks[13], (L, A_WIDTH), 0.02),
        "rwkv_r_k": nrm(ks[14], (L, A_HEADS, HEAD_DIM), 0.1),
        "rwkv_gn_w": 1.0 + nrm(ks[15], (L, A_WIDTH), 0.02),
        "rwkv_gn_b": nrm(ks[16], (L, A_WIDTH), 0.02),
        "band_rel_bias": nrm(ks[17], (L, B_HEADS, 2 * REL_CLIP + 1), 0.5),
        "t5_rel_bias": nrm(ks[18], (T5_BUCKETS, C_HEADS), 0.5),
        "diff_lambda_q1": nrm(ks[19], (L, HEAD_DIM), 0.1),
        "diff_lambda_k1": nrm(ks[20], (L, HEAD_DIM), 0.1),
        "diff_lambda_q2": nrm(ks[21], (L, HEAD_DIM), 0.1),
        "diff_lambda_k2": nrm(ks[22], (L, HEAD_DIM), 0.1),
        "diff_subln_w": 1.0 + nrm(ks[23], (L, 2 * HEAD_DIM), 0.02),
        "conv_w": nrm(ks[24], (L, CONV_WIDTH, D_WIDTH), CONV_WIDTH ** -0.5),
        "router_group_w": nrm(ks[25], (L, D_MODEL, N_GROUPS), D_MODEL ** -0.5),
        "router_group_b": nrm(ks[26], (L, N_GROUPS), 0.01),
        "router_expert_w": nrm(ks[27], (L, D_MODEL, N_EXPERTS), D_MODEL ** -0.5),
        "router_expert_b": nrm(ks[28], (L, N_EXPERTS), 0.01),
        "expert_w_gate": nrm(ks[29], (L, N_EXPERTS, D_MODEL, EXPERT_HIDDEN), D_MODEL ** -0.5),
        "expert_w_up": nrm(ks[30], (L, N_EXPERTS, D_MODEL, EXPERT_HIDDEN), D_MODEL ** -0.5),
        "expert_w_down": nrm(ks[31], (L, N_EXPERTS, EXPERT_HIDDEN, D_MODEL), EXPERT_HIDDEN ** -0.5),
    }


def reference(x, w_in, w_out, norm_mix, norm_ffn, norm_final,
              rwkv_mu, rwkv_w0, rwkv_w2, rwkv_a0, rwkv_a2, rwkv_g2, rwkv_k_k, rwkv_k_a,
              rwkv_r_k, rwkv_gn_w, rwkv_gn_b, band_rel_bias, t5_rel_bias,
              diff_lambda_q1, diff_lambda_k1, diff_lambda_q2, diff_lambda_k2, diff_subln_w,
              conv_w, router_group_w, router_group_b, router_expert_w, router_expert_b,
              expert_w_gate, expert_w_up, expert_w_down):
    bsz, slen, _ = x.shape
    for l in range(DEPTH):
        xn = rmsnorm(x, norm_mix[l])
        p = xn @ w_in[l]
        pA, pB, pC, pD = jnp.split(p, [A_COLS, A_COLS + B_COLS, A_COLS + B_COLS + C_COLS], axis=-1)

        yA = rwkv7_mixer(pA, rwkv_mu[l], rwkv_w0[l], rwkv_w2[l], rwkv_a0[l], rwkv_a2[l],
                         rwkv_g2[l], rwkv_k_k[l], rwkv_k_a[l], rwkv_r_k[l],
                         rwkv_gn_w[l], rwkv_gn_b[l])

        qB, kB, vB = [t.reshape(bsz, slen, B_HEADS, HEAD_DIM) for t in jnp.split(pB, 3, axis=-1)]
        yB = chunk_band_attention(qB, kB, vB, band_rel_bias[l])

        qC, kC, vC = jnp.split(pC, 3, axis=-1)
        qC = qC.reshape(bsz, slen, C_HEADS, 2, HEAD_DIM)
        kC = kC.reshape(bsz, slen, C_HEADS, 2, HEAD_DIM)
        vC = vC.reshape(bsz, slen, C_HEADS, 2 * HEAD_DIM)
        lam_init = 0.8 - 0.6 * math.exp(-0.3 * l)
        lam = (jnp.exp(jnp.sum(diff_lambda_q1[l] * diff_lambda_k1[l]).astype(jnp.float32))
               - jnp.exp(jnp.sum(diff_lambda_q2[l] * diff_lambda_k2[l]).astype(jnp.float32))
               + lam_init)
        yC = diff_attention(qC, kC, vC, t5_rel_bias, lam, lam_init, diff_subln_w[l])

        yD = short_conv_mixer(pD, conv_w[l])

        y = jnp.concatenate([yA, yB.astype(yA.dtype), yC.astype(yA.dtype), yD.astype(yA.dtype)], axis=-1)
        x = x + (y @ w_out[l]).astype(x.dtype)

        x = x + hier_moe(rmsnorm(x, norm_ffn[l]), router_group_w[l], router_group_b[l],
                         router_expert_w[l], router_expert_b[l],
                         expert_w_gate[l], expert_w_up[l], expert_w_down[l])
    return rmsnorm(x, norm_final)
```

```python
import functools
import math

import jax
import jax.numpy as jnp
from jax import lax
from jax.experimental import pallas as pl
from jax.experimental.pallas import tpu as pltpu

F32 = jnp.float32
BF16 = jnp.bfloat16

DEPTH = 2
CHUNK = 64
CHUNK_SHIFT = 6
HEAD_DIM = 64
A_HEADS = 8
B_HEADS = 8
C_HEADS = 4
WIDTH = 512
A_DECAY_LORA = 64
A_ICLR_LORA = 64
A_GATE_LORA = 128
A_LORA = A_DECAY_LORA + A_ICLR_LORA + A_GATE_LORA
A_COLS = 3 * WIDTH + A_LORA
B_OFF = 3 * WIDTH
C_OFF = B_OFF + 3 * WIDTH
D_OFF = C_OFF + 3 * WIDTH
LORA_OFF = D_OFF + 3 * WIDTH
IN_COLS = LORA_OFF + A_LORA
B_LEFT_CHUNKS = 8
B_BAND = (B_LEFT_CHUNKS + 1) * CHUNK
REL_CLIP = 128
T5_BUCKETS = 32
T5_MAX_DIST = 128
N_GROUPS = 4
EXPERTS_PER_GROUP = 8
N_EXPERTS = 32
RMS_EPS = 1e-6
RWKV_GN_EPS = 64e-5
SUBLN_EPS = 1e-5
NEG_INF = -1e30

LANES = 128
VMEM_LIMIT_BYTES = 56 * 1024 * 1024

_NT = (((1,), (1,)), ((), ()))
_TN = (((0,), (0,)), ((), ()))


def _params(semantics):
    return pltpu.CompilerParams(dimension_semantics=semantics,
                                vmem_limit_bytes=VMEM_LIMIT_BYTES)


def _bdot(a, b):
    return jnp.dot(a.astype(BF16), b.astype(BF16), preferred_element_type=F32)


def _sigmoid(x):
    return 1.0 / (1.0 + jnp.exp(-x))


def _inproj_kernel(x_ref, g_ref, w_ref, o_ref, xn_ref):
    @pl.when(pl.program_id(1) == 0)
    def _():
        x = x_ref[...]
        ms = jnp.mean(x * x, axis=-1, keepdims=True)
        xn_ref[...] = (x * lax.rsqrt(ms + RMS_EPS) * g_ref[...]).astype(BF16)

    o_ref[...] = jnp.dot(xn_ref[...], w_ref[...],
                         preferred_element_type=F32).astype(o_ref.dtype)


def _inproj(x2d, gain, w_bf16, tm, tn):
    n, d = x2d.shape
    cols = w_bf16.shape[1]
    return pl.pallas_call(
        _inproj_kernel,
        grid=(n // tm, cols // tn),
        in_specs=[pl.BlockSpec((tm, d), lambda i, j: (i, 0)),
                  pl.BlockSpec((1, d), lambda i, j: (0, 0)),
                  pl.BlockSpec((d, tn), lambda i, j: (0, j))],
        out_specs=pl.BlockSpec((tm, tn), lambda i, j: (i, j)),
        out_shape=jax.ShapeDtypeStruct((n, cols), BF16),
        scratch_shapes=[pltpu.VMEM((tm, d), BF16)],
        compiler_params=_params(("parallel", "arbitrary")),
        name="inproj",
    )(x2d, gain, w_bf16)


def _rwkv_kernel(p_ref, pl_ref, mu_ref, mul_ref, w0_ref, w2_ref, a0_ref, a2_ref, g2_ref,
                 kk_ref, ka_ref, rk_ref, gnw_ref, gnb_ref,
                 o_ref,
                 state_ref, prev_ref, prevl_ref, r_s, k_s, v_s, na_s, b_s, ld_s, cum_s,
                 y_s, g_s, bonus_s):
    tb = p_ref.shape[0]
    n_chunks = tb // CHUNK

    @pl.when(pl.program_id(1) == 0)
    def _():
        state_ref[...] = jnp.zeros_like(state_ref)
        prev_ref[...] = jnp.zeros_like(prev_ref)
        prevl_ref[...] = jnp.zeros_like(prevl_ref)

    row = lax.broadcasted_iota(jnp.int32, (tb, 1), 0)

    def token_shift(src_ref, last_ref, m_ref):
        pa = src_ref[...].astype(F32)
        shifted = jnp.where(row == 0, last_ref[...], pltpu.roll(pa, 1, axis=0))
        last_ref[...] = pa[tb - 1:tb, :]
        return pa + (shifted - pa) * m_ref[...]

    ps = token_shift(p_ref, prev_ref, mu_ref)
    lora = token_shift(pl_ref, prevl_ref, mul_ref)
    r = ps[:, 0:WIDTH]
    k = ps[:, WIDTH:2 * WIDTH]
    v = ps[:, 2 * WIDTH:3 * WIDTH]
    w_lo = lora[:, 0:A_DECAY_LORA]
    a_lo = lora[:, A_DECAY_LORA:A_DECAY_LORA + A_ICLR_LORA]
    g_lo = lora[:, A_DECAY_LORA + A_ICLR_LORA:A_LORA]

    z = -(w0_ref[...] + _bdot(jnp.tanh(w_lo), w2_ref[...]))
    softplus = jnp.maximum(z, 0.0) + jnp.log(1.0 + jnp.exp(-jnp.abs(z)))
    ld = -jnp.exp(-softplus - 0.5)
    a = _sigmoid(a0_ref[...] + _bdot(a_lo, a2_ref[...]))
    g_s[...] = _bdot(_sigmoid(g_lo), g2_ref[...])

    kk = k * kk_ref[...]
    kmod = k * (1.0 + (a - 1.0) * ka_ref[...])
    rkr = r * kmod * rk_ref[...]
    for h in range(A_HEADS):
        sl = slice(h * HEAD_DIM, (h + 1) * HEAD_DIM)
        kkh = kk[:, sl]
        nrm = jnp.sqrt(jnp.sum(kkh * kkh, axis=-1, keepdims=True))
        kkn = kkh / jnp.maximum(nrm, 1e-12)
        na_s[:, sl] = -kkn
        b_s[:, sl] = kkn * a[:, sl]
        bonus_s[:, sl] = jnp.sum(rkr[:, sl], axis=-1, keepdims=True) * v[:, sl]
    r_s[...] = r
    k_s[...] = kmod
    v_s[...] = v
    ld_s[...] = ld

    ri = lax.broadcasted_iota(jnp.int32, (tb, tb), 0)
    ci = lax.broadcasted_iota(jnp.int32, (tb, tb), 1)
    same_chunk = (lax.shift_right_logical(ri, CHUNK_SHIFT)
                  == lax.shift_right_logical(ci, CHUNK_SHIFT))
    tri = jnp.where((ri >= ci) & same_chunk, 1.0, 0.0).astype(BF16)
    ld_hi = ld.astype(BF16)
    ld_lo = (ld - ld_hi.astype(F32)).astype(BF16)
    cum_s[...] = (jnp.dot(tri, ld_hi, preferred_element_type=F32)
                  + jnp.dot(tri, ld_lo, preferred_element_type=F32))

    rc = lax.broadcasted_iota(jnp.int32, (CHUNK, CHUNK), 0)
    cc = lax.broadcasted_iota(jnp.int32, (CHUNK, CHUNK), 1)
    strict = rc > cc
    incl = rc >= cc
    eye = jnp.where(rc == cc, 1.0, 0.0)

    def chunk_body(c, carry):
        r0 = pl.multiple_of(c * CHUNK, CHUNK)
        rows = pl.ds(r0, CHUNK)
        for h in range(A_HEADS):
            sl = slice(h * HEAD_DIM, (h + 1) * HEAD_DIM)
            rh = r_s[rows, sl]
            kh = k_s[rows, sl]
            vh = v_s[rows, sl]
            nah = na_s[rows, sl]
            bh = b_s[rows, sl]
            ldh = ld_s[rows, sl]
            cum = cum_s[rows, sl]
            cum_last = cum[CHUNK - 1:CHUNK, :]
            w_in = jnp.exp(cum)
            w_inv = jnp.exp(-cum)
            w_end = jnp.exp(cum_last - cum)
            wc = jnp.exp(cum_last)
            at = nah * jnp.exp(cum - ldh)
            rt = rh * w_in
            ar = jnp.concatenate([at, rt], axis=0).astype(BF16)
            bk = jnp.concatenate([bh * w_inv, kh * w_inv], axis=0).astype(BF16)
            gram = lax.dot_general(ar, bk, _NT, preferred_element_type=F32)
            l_ab = jnp.where(strict, gram[0:CHUNK, 0:CHUNK], 0.0)
            l_ak = jnp.where(strict, gram[0:CHUNK, CHUNK:], 0.0)
            m_rb = jnp.where(incl, gram[CHUNK:, 0:CHUNK], 0.0)
            m_rk = jnp.where(incl, gram[CHUNK:, CHUNK:], 0.0)
            tinv = eye + l_ab
            mpow = l_ab
            for _ in range(5):
                mpow = _bdot(mpow, mpow)
                tinv = tinv + _bdot(mpow, tinv)
            s0 = state_ref[h]
            ars = lax.dot_general(ar, s0.astype(BF16), _NT, preferred_element_type=F32)
            x = ars[0:CHUNK] + _bdot(l_ak, vh)
            u = _bdot(tinv, x)
            uv = jnp.concatenate([u, vh], axis=0).astype(BF16)
            m_cat = jnp.concatenate([m_rb, m_rk], axis=1).astype(BF16)
            y = ars[CHUNK:] + jnp.dot(m_cat, uv, preferred_element_type=F32)
            bk_end = jnp.concatenate([bh * w_end, kh * w_end], axis=0).astype(BF16)
            state_ref[h] = s0 * wc + lax.dot_general(uv, bk_end, _TN,
                                                     preferred_element_type=F32)
            mean = jnp.mean(y, axis=-1, keepdims=True)
            yc = y - mean
            var = jnp.mean(yc * yc, axis=-1, keepdims=True)
            y_s[rows, sl] = yc * lax.rsqrt(var + RWKV_GN_EPS)
        return carry

    lax.fori_loop(0, n_chunks, chunk_body, 0)

    out = (y_s[...] * gnw_ref[...] + gnb_ref[...] + bonus_s[...]) * g_s[...]
    o_ref[...] = out.astype(o_ref.dtype)


def _rwkv(p, nb, seq, tb, mu, w0, w2, a0, a2, g2, k_k, k_a, r_k, gn_w, gn_b):
    n = p.shape[0]
    nt = seq // tb
    row2 = lambda t: t.reshape(1, -1).astype(F32)
    vec_spec = lambda width: pl.BlockSpec((1, width), lambda b, t: (0, 0))
    full = lambda arr: pl.BlockSpec(arr.shape, lambda b, t: (0, 0))
    w2b, a2b, g2b = w2.astype(BF16), a2.astype(BF16), g2.astype(BF16)
    scr = lambda: pltpu.VMEM((tb, WIDTH), F32)
    return pl.pallas_call(
        _rwkv_kernel,
        grid=(nb, nt),
        in_specs=[pl.BlockSpec((tb, 3 * WIDTH), lambda b, t: (b * nt + t, 0)),
                  pl.BlockSpec((tb, A_LORA), lambda b, t: (b * nt + t, LORA_OFF // A_LORA)),
                  vec_spec(3 * WIDTH), vec_spec(A_LORA), vec_spec(WIDTH), full(w2b),
                  vec_spec(WIDTH), full(a2b), full(g2b), vec_spec(WIDTH), vec_spec(WIDTH),
                  vec_spec(WIDTH), vec_spec(WIDTH), vec_spec(WIDTH)],
        out_specs=pl.BlockSpec((tb, WIDTH), lambda b, t: (b * nt + t, 0)),
        out_shape=jax.ShapeDtypeStruct((n, WIDTH), BF16),
        scratch_shapes=[pltpu.VMEM((A_HEADS, HEAD_DIM, HEAD_DIM), F32),
                        pltpu.VMEM((1, 3 * WIDTH), F32),
                        pltpu.VMEM((1, A_LORA), F32)] + [scr() for _ in range(10)],
        compiler_params=_params(("parallel", "arbitrary")),
        name="rwkv",
    )(p, p, row2(mu[:3 * WIDTH]), row2(mu[3 * WIDTH:]), row2(w0), w2b, row2(a0), a2b, g2b,
      row2(k_k), row2(k_a), row2(r_k), row2(gn_w), row2(gn_b))


def _band_kernel(q_ref, kp_ref, kc_ref, vp_ref, vc_ref, bias_ref, o_ref, kbuf, vbuf):
    tq = q_ref.shape[0]
    i = pl.program_id(1)
    kbuf[0:tq, :] = kp_ref[...]
    kbuf[tq:2 * tq, :] = kc_ref[...]
    vbuf[0:tq, :] = vp_ref[...]
    vbuf[tq:2 * tq, :] = vc_ref[...]
    col = lax.broadcasted_iota(jnp.int32, (1, B_BAND), 1)
    scale = HEAD_DIM ** -0.5

    def chunk_body(c, carry):
        r0 = pl.multiple_of(c * CHUNK, CHUNK)
        valid = jnp.logical_or(i > 0, col + r0 >= tq)
        for h in range(B_HEADS):
            sl = slice(h * HEAD_DIM, (h + 1) * HEAD_DIM)
            q = q_ref[pl.ds(r0, CHUNK), sl]
            kw = kbuf[pl.ds(r0, B_BAND), sl]
            vw = vbuf[pl.ds(r0, B_BAND), sl]
            s = lax.dot_general(q, kw, _NT, preferred_element_type=F32) * scale + bias_ref[h]
            s = jnp.where(valid, s, NEG_INF)
            m = jnp.max(s, axis=-1, keepdims=True)
            e = jnp.exp(s - m)
            l = jnp.sum(e, axis=-1, keepdims=True)
            pv = jnp.dot(e.astype(BF16), vw, preferred_element_type=F32)
            o_ref[pl.ds(r0, CHUNK), sl] = (pv / l).astype(o_ref.dtype)
        return carry

    lax.fori_loop(0, tq // CHUNK, chunk_body, 0)


def _band(p, nb, seq, tq, bias):
    n = p.shape[0]
    nt = seq // tq
    qb, kb, vb = B_OFF // WIDTH, B_OFF // WIDTH + 1, B_OFF // WIDTH + 2
    cur = lambda cb: pl.BlockSpec((tq, WIDTH), lambda b, t: (b * nt + t, cb))
    prv = lambda cb: pl.BlockSpec((tq, WIDTH), lambda b, t: (b * nt + jnp.maximum(t - 1, 0), cb))
    return pl.pallas_call(
        _band_kernel,
        grid=(nb, nt),
        in_specs=[cur(qb), prv(kb), cur(kb), prv(vb), cur(vb),
                  pl.BlockSpec(bias.shape, lambda b, t: (0, 0, 0))],
        out_specs=pl.BlockSpec((tq, WIDTH), lambda b, t: (b * nt + t, 0)),
        out_shape=jax.ShapeDtypeStruct((n, WIDTH), BF16),
        scratch_shapes=[pltpu.VMEM((2 * tq, WIDTH), BF16),
                        pltpu.VMEM((2 * tq, WIDTH), BF16)],
        compiler_params=_params(("parallel", "arbitrary")),
        name="band",
    )(p, p, p, p, p, bias)


def _diff_kernel(q_ref, k_ref, v_ref, bd_ref, bp_ref, far_ref,
                 lq1_ref, lk1_ref, lq2_ref, lk2_ref, sub_ref, o_ref,
                 m_s, l_s, acc_s, *, lam_init):
    tq = q_ref.shape[0]
    i = pl.program_id(2)
    scale = HEAD_DIM ** -0.5
    q = q_ref[...]
    qs = (q[:, 0:HEAD_DIM], q[:, HEAD_DIM:])

    def scores(kt, comp):
        kc = kt[:, comp * HEAD_DIM:(comp + 1) * HEAD_DIM]
        return lax.dot_general(qs[comp], kc, _NT, preferred_element_type=F32) * scale

    def first_tile(kt, vt, bias):
        for comp in range(2):
            s = scores(kt, comp) + bias
            m = jnp.max(s, axis=-1, keepdims=True)
            e = jnp.exp(s - m)
            m_s[comp] = m
            l_s[comp] = jnp.sum(e, axis=-1, keepdims=True)
            acc_s[comp] = jnp.dot(e.astype(BF16), vt, preferred_element_type=F32)

    def next_tile(kt, vt, bias):
        for comp in range(2):
            s = scores(kt, comp) + bias
            m_old = m_s[comp]
            m_new = jnp.maximum(m_old, jnp.max(s, axis=-1, keepdims=True))
            alpha = jnp.exp(m_old - m_new)
            e = jnp.exp(s - m_new)
            m_s[comp] = m_new
            l_s[comp] = alpha * l_s[comp] + jnp.sum(e, axis=-1, keepdims=True)
            acc_s[comp] = alpha * acc_s[comp] + jnp.dot(e.astype(BF16), vt,
                                                        preferred_element_type=F32)

    def tile(j):
        r0 = pl.multiple_of(j * tq, tq)
        return k_ref[pl.ds(r0, tq), :], v_ref[pl.ds(r0, tq), :]

    kt, vt = tile(i)
    first_tile(kt, vt, bd_ref[...])

    @pl.when(i >= 1)
    def _():
        kt, vt = tile(i - 1)
        next_tile(kt, vt, bp_ref[...])

    far = far_ref[0:1, 0:1]

    def far_body(j, carry):
        kt, vt = tile(j)
        next_tile(kt, vt, far)
        return carry

    lax.fori_loop(0, jnp.maximum(i - 1, 0), far_body, 0)

    lam = (jnp.exp(jnp.sum(lq1_ref[...] * lk1_ref[...], axis=-1, keepdims=True))
           - jnp.exp(jnp.sum(lq2_ref[...] * lk2_ref[...], axis=-1, keepdims=True))
           + lam_init)
    out = acc_s[0] / l_s[0] - lam * (acc_s[1] / l_s[1])
    out = out * lax.rsqrt(jnp.mean(out * out, axis=-1, keepdims=True) + SUBLN_EPS)
    out = out * sub_ref[...] * (1.0 - lam_init)
    o_ref[...] = out.astype(o_ref.dtype)


def _diff(p, nb, seq, tq, bias_diag, bias_prev, bias_far, lq1, lk1, lq2, lk2, subw, lam_init):
    n = p.shape[0]
    nt = seq // tq
    hw = 2 * HEAD_DIM
    qb, kb, vb = C_OFF // hw, (C_OFF + WIDTH) // hw, (C_OFF + 2 * WIDTH) // hw
    row2 = lambda t: t.reshape(1, -1).astype(F32)
    vec = lambda width: pl.BlockSpec((1, width), lambda b, h, t: (0, 0))
    return pl.pallas_call(
        functools.partial(_diff_kernel, lam_init=lam_init),
        grid=(nb, C_HEADS, nt),
        in_specs=[pl.BlockSpec((tq, hw), lambda b, h, t: (b * nt + t, qb + h)),
                  pl.BlockSpec((seq, hw), lambda b, h, t: (b, kb + h)),
                  pl.BlockSpec((seq, hw), lambda b, h, t: (b, vb + h)),
                  pl.BlockSpec((None, tq, tq), lambda b, h, t: (h, 0, 0)),
                  pl.BlockSpec((None, tq, tq), lambda b, h, t: (h, 0, 0)),
                  pl.BlockSpec((None, 1, LANES), lambda b, h, t: (h, 0, 0)),
                  vec(HEAD_DIM), vec(HEAD_DIM), vec(HEAD_DIM), vec(HEAD_DIM), vec(hw)],
        out_specs=pl.BlockSpec((tq, hw), lambda b, h, t: (b * nt + t, h)),
        out_shape=jax.ShapeDtypeStruct((n, WIDTH), BF16),
        scratch_shapes=[pltpu.VMEM((2, tq, 1), F32), pltpu.VMEM((2, tq, 1), F32),
                        pltpu.VMEM((2, tq, hw), F32)],
        compiler_params=_params(("parallel", "parallel", "arbitrary")),
        name="diffattn",
    )(p, p, p, bias_diag, bias_prev, bias_far, row2(lq1), row2(lk1), row2(lq2),
      row2(lk2), row2(subw))


def _t5_buckets(rel):
    nb = T5_BUCKETS // 2
    max_exact = nb // 2
    ret = (rel > 0).astype(jnp.int32) * nb
    n = jnp.abs(rel)
    nf = jnp.maximum(n, 1).astype(jnp.float32)
    large = max_exact + (jnp.log(nf / max_exact) / math.log(T5_MAX_DIST / max_exact)
                         * (nb - max_exact)).astype(jnp.int32)
    large = jnp.minimum(large, nb - 1)
    return ret + jnp.where(n < max_exact, n, large)


def _t5_tiles(t5_table, tq):
    qi = jnp.arange(tq)[:, None]
    ki = jnp.arange(tq)[None, :]
    tab = t5_table.astype(F32)
    diag = jnp.moveaxis(tab[_t5_buckets(ki - qi)], -1, 0)
    diag = jnp.where((ki // CHUNK) <= (qi // CHUNK), diag, NEG_INF)
    prev = jnp.moveaxis(tab[_t5_buckets(ki - tq - qi)], -1, 0)
    far = tab[_t5_buckets(jnp.int32(-2 * tq))]
    far = jnp.broadcast_to(far[:, None, None], (C_HEADS, 1, LANES))
    return diag, prev, far


def _outproj_kernel(ya_ref, yb_ref, yc_ref, pb_ref, pc_ref, ph_ref, pcp_ref, php_ref,
                    cw_ref, w_ref, x_ref, o_ref, *, tiles_per_seq):
    tm = x_ref.shape[0]
    i = pl.program_id(0)
    u = pc_ref[...].astype(F32) * ph_ref[...].astype(F32)
    up = pcp_ref[...].astype(F32) * php_ref[...].astype(F32)
    up = jnp.where(i % tiles_per_seq == 0, 0.0, up)
    row = lax.broadcasted_iota(jnp.int32, (tm, 1), 0)
    s1 = jnp.where(row == 0, up[7:8, :], pltpu.roll(u, 1, axis=0))
    s2 = pltpu.roll(u, 2, axis=0)
    s2 = jnp.where(row == 0, up[6:7, :], jnp.where(row == 1, up[7:8, :], s2))
    cw = cw_ref[...]
    yd = pb_ref[...].astype(F32) * (cw[0:1, :] * s2 + cw[1:2, :] * s1 + cw[2:3, :] * u)
    acc = jnp.dot(ya_ref[...], w_ref[0:WIDTH, :], preferred_element_type=F32)
    acc += jnp.dot(yb_ref[...], w_ref[WIDTH:2 * WIDTH, :], preferred_element_type=F32)
    acc += jnp.dot(yc_ref[...], w_ref[2 * WIDTH:3 * WIDTH, :], preferred_element_type=F32)
    acc += jnp.dot(yd.astype(BF16), w_ref[3 * WIDTH:, :], preferred_element_type=F32)
    o_ref[...] = x_ref[...] + acc


def _outproj(ya, yb, yc, p, conv_w, w_bf16, x2d, seq, tm):
    n, d = x2d.shape
    db = D_OFF // WIDTH
    r8 = tm // 8
    ycur = pl.BlockSpec((tm, WIDTH), lambda i: (i, 0))
    pcur = lambda cb: pl.BlockSpec((tm, WIDTH), lambda i: (i, cb))
    pprev = lambda cb: pl.BlockSpec((8, WIDTH), lambda i: (jnp.maximum(i * r8 - 1, 0), cb))
    return pl.pallas_call(
        functools.partial(_outproj_kernel, tiles_per_seq=seq // tm),
        grid=(n // tm,),
        in_specs=[ycur, ycur, ycur, pcur(db), pcur(db + 1), pcur(db + 2),
                  pprev(db + 1), pprev(db + 2),
                  pl.BlockSpec(conv_w.shape, lambda i: (0, 0)),
                  pl.BlockSpec(w_bf16.shape, lambda i: (0, 0)),
                  pl.BlockSpec((tm, d), lambda i: (i, 0))],
        out_specs=pl.BlockSpec((tm, d), lambda i: (i, 0)),
        out_shape=jax.ShapeDtypeStruct((n, d), F32),
        compiler_params=_params(("parallel",)),
        name="outproj",
    )(ya, yb, yc, p, p, p, p, p, conv_w.astype(F32), w_bf16, x2d)


def _router_kernel(x_ref, g_ref, wh_ref, wl_ref, b_ref, xn_ref, comb_ref):
    x = x_ref[...]
    ms = jnp.mean(x * x, axis=-1, keepdims=True)
    xn = x * lax.rsqrt(ms + RMS_EPS) * g_ref[...]
    xh = xn.astype(BF16)
    xn_ref[...] = xh
    xl = (xn - xh.astype(F32)).astype(BF16)
    logits = (jnp.dot(xh, wh_ref[...], preferred_element_type=F32)
              + jnp.dot(xh, wl_ref[...], preferred_element_type=F32)
              + jnp.dot(xl, wh_ref[...], preferred_element_type=F32)) + b_ref[...]
    tm = x.shape[0]
    lane_i = lax.broadcasted_iota(jnp.int32, (tm, LANES), 1)
    lane = lane_i.astype(F32)
    lane_grp = lax.shift_right_logical(lane_i, 3).astype(F32)
    is_g = (lane_i >= N_EXPERTS) & (lane_i < N_EXPERTS + N_GROUPS)
    gl = jnp.where(is_g, logits, NEG_INF)
    gmax = jnp.max(gl, axis=-1, keepdims=True)
    gsum = jnp.sum(jnp.where(is_g, jnp.exp(gl - gmax), 0.0), axis=-1, keepdims=True)
    g_w = 1.0 / gsum
    g_idx = jnp.min(jnp.where(is_g & (gl == gmax), lane, float(LANES)), axis=-1,
                    keepdims=True) - float(N_EXPERTS)
    in_grp = (lane_i < N_EXPERTS) & (lane_grp == g_idx)
    el = jnp.where(in_grp, logits, NEG_INF)
    emax = jnp.max(el, axis=-1, keepdims=True)
    ee = jnp.where(in_grp, jnp.exp(el - emax), 0.0)
    ep = ee / jnp.sum(ee, axis=-1, keepdims=True)
    p1 = jnp.max(ep, axis=-1, keepdims=True)
    i1 = jnp.min(jnp.where(in_grp & (ep == p1), lane, float(LANES)), axis=-1, keepdims=True)
    rest = in_grp & (lane != i1)
    ep2 = jnp.where(rest, ep, -1.0)
    p2 = jnp.max(ep2, axis=-1, keepdims=True)
    i2 = jnp.min(jnp.where(rest & (ep2 == p2), lane, float(LANES)), axis=-1, keepdims=True)
    tot = p1 + p2
    comb = jnp.where(lane == i1, p1 / tot, 0.0) + jnp.where(lane == i2, p2 / tot, 0.0)
    comb_ref[...] = comb * g_w


def _router(x2d, gain, wr_hi, wr_lo, br, tm):
    n, d = x2d.shape
    return pl.pallas_call(
        _router_kernel,
        grid=(n // tm,),
        in_specs=[pl.BlockSpec((tm, d), lambda i: (i, 0)),
                  pl.BlockSpec((1, d), lambda i: (0, 0)),
                  pl.BlockSpec((d, LANES), lambda i: (0, 0)),
                  pl.BlockSpec((d, LANES), lambda i: (0, 0)),
                  pl.BlockSpec((1, LANES), lambda i: (0, 0))],
        out_specs=[pl.BlockSpec((tm, d), lambda i: (i, 0)),
                   pl.BlockSpec((tm, LANES), lambda i: (i, 0))],
        out_shape=[jax.ShapeDtypeStruct((n, d), BF16),
                   jax.ShapeDtypeStruct((n, LANES), F32)],
        compiler_params=_params(("parallel",)),
        name="router",
    )(x2d, gain, wr_hi, wr_lo, br)


def _experts_kernel(x_ref, xn_ref, comb_ref, wg_ref, wu_ref, wd_ref, gf_ref, o_ref,
                    acc_ref, *, final_norm):
    e = pl.program_id(1)

    @pl.when(e == 0)
    def _():
        acc_ref[...] = jnp.zeros_like(acc_ref)

    xn = xn_ref[...]
    hg = jnp.dot(xn, wg_ref[...], preferred_element_type=F32)
    hu = jnp.dot(xn, wu_ref[...], preferred_element_type=F32)
    lane = lax.broadcasted_iota(jnp.int32, comb_ref.shape, 1)
    cw = jnp.sum(jnp.where(lane == e, comb_ref[...], 0.0), axis=-1, keepdims=True)
    h = hg * _sigmoid(hg) * hu * cw
    acc_ref[...] += jnp.dot(h.astype(BF16), wd_ref[...], preferred_element_type=F32)

    @pl.when(e == pl.num_programs(1) - 1)
    def _():
        y = x_ref[...] + acc_ref[...]
        if final_norm:
            ms = jnp.mean(y * y, axis=-1, keepdims=True)
            y = y * lax.rsqrt(ms + RMS_EPS) * gf_ref[...]
        o_ref[...] = y


def _experts(x2d, xn, comb, wg, wu, wd, gain_final, final_norm, tm):
    n, d = x2d.shape
    ne, _, hid = wg.shape
    return pl.pallas_call(
        functools.partial(_experts_kernel, final_norm=final_norm),
        grid=(n // tm, ne),
        in_specs=[pl.BlockSpec((tm, d), lambda i, e: (i, 0)),
                  pl.BlockSpec((tm, d), lambda i, e: (i, 0)),
                  pl.BlockSpec((tm, LANES), lambda i, e: (i, 0)),
                  pl.BlockSpec((None, d, hid), lambda i, e: (e, 0, 0)),
                  pl.BlockSpec((None, d, hid), lambda i, e: (e, 0, 0)),
                  pl.BlockSpec((None, hid, d), lambda i, e: (e, 0, 0)),
                  pl.BlockSpec((1, d), lambda i, e: (0, 0))],
        out_specs=pl.BlockSpec((tm, d), lambda i, e: (i, 0)),
        out_shape=jax.ShapeDtypeStruct((n, d), F32),
        scratch_shapes=[pltpu.VMEM((tm, d), F32)],
        compiler_params=_params(("parallel", "arbitrary")),
        name="experts",
    )(x2d, xn, comb, wg, wu, wd, gain_final)


def _pick(total, pref):
    t = min(pref, total)
    while total % t:
        t //= 2
    return t


def kernel(x, w_in, w_out, norm_mix, norm_ffn, norm_final, rwkv_mu, rwkv_w0, rwkv_w2, rwkv_a0, rwkv_a2, rwkv_g2, rwkv_k_k, rwkv_k_a, rwkv_r_k, rwkv_gn_w, rwkv_gn_b, band_rel_bias, t5_rel_bias, diff_lambda_q1, diff_lambda_k1, diff_lambda_q2, diff_lambda_k2, diff_subln_w, conv_w, router_group_w, router_group_b, router_expert_w, router_expert_b, expert_w_gate, expert_w_up, expert_w_down):
    nb, seq, d = x.shape
    n = nb * seq
    depth = w_in.shape[0]
    x2d = x.reshape(n, d)

    tm_proj = _pick(n, 1024)
    tm_moe = _pick(n, 512)
    t_rwkv = _pick(seq, 256)
    t_band = _pick(seq, 512)
    t_diff = _pick(seq, 256)
    row2 = lambda t: t.reshape(1, -1).astype(F32)

    pad = B_LEFT_CHUNKS * CHUNK
    dist = (pad + jnp.arange(CHUNK)[:, None]) - jnp.arange(B_BAND)[None, :]
    band_idx = jnp.clip(dist, -REL_CLIP, REL_CLIP) + REL_CLIP
    t5_diag, t5_prev, t5_far = _t5_tiles(t5_rel_bias, t_diff)

    for l in range(depth):
        wl = w_in[l]
        w_perm = jnp.concatenate([wl[:, :3 * WIDTH], wl[:, A_COLS:], wl[:, 3 * WIDTH:A_COLS]],
                                 axis=1).astype(BF16)
        p = _inproj(x2d, row2(norm_mix[l]), w_perm, tm_proj, 1280)

        ya = _rwkv(p, nb, seq, t_rwkv, rwkv_mu[l], rwkv_w0[l], rwkv_w2[l], rwkv_a0[l],
                   rwkv_a2[l], rwkv_g2[l], rwkv_k_k[l], rwkv_k_a[l], rwkv_r_k[l],
                   rwkv_gn_w[l], rwkv_gn_b[l])
        yb = _band(p, nb, seq, t_band, band_rel_bias[l][:, band_idx].astype(F32))
        lam_init = 0.8 - 0.6 * math.exp(-0.3 * l)
        yc = _diff(p, nb, seq, t_diff, t5_diag, t5_prev, t5_far,
                   diff_lambda_q1[l], diff_lambda_k1[l], diff_lambda_q2[l],
                   diff_lambda_k2[l], diff_subln_w[l], lam_init)
        x2d = _outproj(ya, yb, yc, p, conv_w[l], w_out[l].astype(BF16), x2d, seq, tm_moe)

        wr = jnp.concatenate([router_expert_w[l], router_group_w[l]], axis=1).astype(F32)
        wr = jnp.pad(wr, ((0, 0), (0, LANES - wr.shape[1])))
        wr_hi = wr.astype(BF16)
        wr_lo = (wr - wr_hi.astype(F32)).astype(BF16)
        br = jnp.concatenate([router_expert_b[l], router_group_b[l]]).astype(F32)
        br = jnp.pad(br, (0, LANES - br.shape[0])).reshape(1, LANES)
        xn, comb = _router(x2d, row2(norm_ffn[l]), wr_hi, wr_lo, br, tm_moe)
        x2d = _experts(x2d, xn, comb, expert_w_gate[l].astype(BF16),
                       expert_w_up[l].astype(BF16), expert_w_down[l].astype(BF16),
                       row2(norm_final), l == depth - 1, tm_moe)
    return x2d.reshape(nb, seq, d)
```

```python
import functools
import math

import jax
import jax.numpy as jnp
from jax import lax
from jax.experimental import pallas as pl
from jax.experimental.pallas import tpu as pltpu

F32 = jnp.float32
BF16 = jnp.bfloat16

DEPTH = 2
CHUNK = 64
CHUNK_SHIFT = 6
HEAD_DIM = 64
A_HEADS = 8
B_HEADS = 8
C_HEADS = 4
WIDTH = 512
A_DECAY_LORA = 64
A_ICLR_LORA = 64
A_GATE_LORA = 128
A_LORA = A_DECAY_LORA + A_ICLR_LORA + A_GATE_LORA
A_COLS = 3 * WIDTH + A_LORA
B_OFF = 3 * WIDTH
C_OFF = B_OFF + 3 * WIDTH
D_OFF = C_OFF + 3 * WIDTH
LORA_OFF = D_OFF + 3 * WIDTH
IN_COLS = LORA_OFF + A_LORA
B_LEFT_CHUNKS = 8
B_BAND = (B_LEFT_CHUNKS + 1) * CHUNK
REL_CLIP = 128
T5_BUCKETS = 32
T5_MAX_DIST = 128
N_GROUPS = 4
EXPERTS_PER_GROUP = 8
N_EXPERTS = 32
RMS_EPS = 1e-6
RWKV_GN_EPS = 64e-5
SUBLN_EPS = 1e-5
NEG_INF = -1e30

LANES = 128
BF16_SUBLANES = 16
VMEM_LIMIT_BYTES = 56 * 1024 * 1024

_NT = (((1,), (1,)), ((), ()))
_TN = (((0,), (0,)), ((), ()))


def _params(semantics):
    return pltpu.CompilerParams(dimension_semantics=semantics,
                                vmem_limit_bytes=VMEM_LIMIT_BYTES)


def _bdot(a, b):
    return jnp.dot(a.astype(BF16), b.astype(BF16), preferred_element_type=F32)


def _sigmoid(x):
    return 1.0 / (1.0 + jnp.exp(-x))


def _inproj_kernel(x_ref, g_ref, w_ref, o_ref, xn_ref):
    @pl.when(pl.program_id(1) == 0)
    def _():
        x = x_ref[...]
        ms = jnp.mean(x * x, axis=-1, keepdims=True)
        xn_ref[...] = (x * lax.rsqrt(ms + RMS_EPS) * g_ref[...]).astype(BF16)

    o_ref[...] = jnp.dot(xn_ref[...], w_ref[...],
                         preferred_element_type=F32).astype(o_ref.dtype)


def _inproj(x2d, gain, w_bf16, tm, tn):
    n, d = x2d.shape
    cols = w_bf16.shape[1]
    return pl.pallas_call(
        _inproj_kernel,
        grid=(n // tm, cols // tn),
        in_specs=[pl.BlockSpec((tm, d), lambda i, j: (i, 0)),
                  pl.BlockSpec((1, d), lambda i, j: (0, 0)),
                  pl.BlockSpec((d, tn), lambda i, j: (0, j))],
        out_specs=pl.BlockSpec((tm, tn), lambda i, j: (i, j)),
        out_shape=jax.ShapeDtypeStruct((n, cols), BF16),
        scratch_shapes=[pltpu.VMEM((tm, d), BF16)],
        compiler_params=_params(("parallel", "arbitrary")),
        name="inproj",
    )(x2d, gain, w_bf16)


def _rwkv_kernel(p_ref, pl_ref, mu_ref, mul_ref, w0_ref, w2_ref, a0_ref, a2_ref, g2_ref,
                 kk_ref, ka_ref, rk_ref, gnw_ref, gnb_ref,
                 o_ref,
                 state_ref, prev_ref, prevl_ref, r_s, k_s, v_s, na_s, b_s, ld_s, cum_s,
                 y_s, g_s, bonus_s, x0_s, y0_s, mrb_s, tinv_s, ar_s, bke_s):
    tb = p_ref.shape[0]
    n_chunks = tb // CHUNK

    @pl.when(pl.program_id(1) == 0)
    def _():
        state_ref[...] = jnp.zeros_like(state_ref)
        prev_ref[...] = jnp.zeros_like(prev_ref)
        prevl_ref[...] = jnp.zeros_like(prevl_ref)

    row = lax.broadcasted_iota(jnp.int32, (tb, 1), 0)

    def token_shift(src_ref, last_ref, m_ref):
        pa = src_ref[...].astype(F32)
        shifted = jnp.where(row == 0, last_ref[...], pltpu.roll(pa, 1, axis=0))
        last_ref[...] = pa[tb - 1:tb, :]
        return pa + (shifted - pa) * m_ref[...]

    ps = token_shift(p_ref, prev_ref, mu_ref)
    lora = token_shift(pl_ref, prevl_ref, mul_ref)
    r = ps[:, 0:WIDTH]
    k = ps[:, WIDTH:2 * WIDTH]
    v = ps[:, 2 * WIDTH:3 * WIDTH]
    w_lo = lora[:, 0:A_DECAY_LORA]
    a_lo = lora[:, A_DECAY_LORA:A_DECAY_LORA + A_ICLR_LORA]
    g_lo = lora[:, A_DECAY_LORA + A_ICLR_LORA:A_LORA]

    z = -(w0_ref[...] + _bdot(jnp.tanh(w_lo), w2_ref[...]))
    softplus = jnp.maximum(z, 0.0) + jnp.log(1.0 + jnp.exp(-jnp.abs(z)))
    ld = -jnp.exp(-softplus - 0.5)
    a = _sigmoid(a0_ref[...] + _bdot(a_lo, a2_ref[...]))
    g_s[...] = _bdot(_sigmoid(g_lo), g2_ref[...])

    kk = k * kk_ref[...]
    kmod = k * (1.0 + (a - 1.0) * ka_ref[...])
    rkr = r * kmod * rk_ref[...]
    for h in range(A_HEADS):
        sl = slice(h * HEAD_DIM, (h + 1) * HEAD_DIM)
        kkh = kk[:, sl]
        nrm = jnp.sqrt(jnp.sum(kkh * kkh, axis=-1, keepdims=True))
        kkn = kkh / jnp.maximum(nrm, 1e-12)
        na_s[:, sl] = -kkn
        b_s[:, sl] = kkn * a[:, sl]
        bonus_s[:, sl] = jnp.sum(rkr[:, sl], axis=-1, keepdims=True) * v[:, sl]
    r_s[...] = r
    k_s[...] = kmod
    v_s[...] = v
    ld_s[...] = ld

    ri = lax.broadcasted_iota(jnp.int32, (tb, tb), 0)
    ci = lax.broadcasted_iota(jnp.int32, (tb, tb), 1)
    same_chunk = (lax.shift_right_logical(ri, CHUNK_SHIFT)
                  == lax.shift_right_logical(ci, CHUNK_SHIFT))
    tri = jnp.where((ri >= ci) & same_chunk, 1.0, 0.0).astype(BF16)
    ld_hi = ld.astype(BF16)
    ld_lo = (ld - ld_hi.astype(F32)).astype(BF16)
    cum_s[...] = (jnp.dot(tri, ld_hi, preferred_element_type=F32)
                  + jnp.dot(tri, ld_lo, preferred_element_type=F32))

    rc = lax.broadcasted_iota(jnp.int32, (CHUNK, CHUNK), 0)
    cc = lax.broadcasted_iota(jnp.int32, (CHUNK, CHUNK), 1)
    strict = rc > cc
    incl = rc >= cc
    eye = jnp.where(rc == cc, 1.0, 0.0)

    heads = range(A_HEADS)
    hsl = [slice(h * HEAD_DIM, (h + 1) * HEAD_DIM) for h in heads]

    def phase_a(c, carry):
        r0 = pl.multiple_of(c * CHUNK, CHUNK)
        rows = pl.ds(r0, CHUNK)
        cum = cum_s[rows, :]
        cum_last = cum[CHUNK - 1:CHUNK, :]
        w_inv = jnp.exp(-cum)
        w_end = jnp.exp(cum_last - cum)
        at = na_s[rows, :] * jnp.exp(cum - ld_s[rows, :])
        rt = r_s[rows, :] * jnp.exp(cum)
        bh = b_s[rows, :]
        kh = k_s[rows, :]
        vh = v_s[rows, :].astype(BF16)
        bt, kt = bh * w_inv, kh * w_inv
        be, ke = bh * w_end, kh * w_end
        ar = [jnp.concatenate([at[:, s], rt[:, s]], axis=0).astype(BF16) for s in hsl]
        bk = [jnp.concatenate([bt[:, s], kt[:, s]], axis=0).astype(BF16) for s in hsl]
        gram = [lax.dot_general(ar[h], bk[h], _NT, preferred_element_type=F32) for h in heads]
        l_ab = [jnp.where(strict, g[0:CHUNK, 0:CHUNK], 0.0) for g in gram]
        tinv = [eye + m for m in l_ab]
        mpow = l_ab
        for _ in range(5):
            mpow = [_bdot(m, m) for m in mpow]
            tinv = [t + _bdot(m, t) for m, t in zip(mpow, tinv)]
        for h in heads:
            idx = c * A_HEADS + h
            g = gram[h]
            vhh = vh[:, hsl[h]]
            l_ak = jnp.where(strict, g[0:CHUNK, CHUNK:], 0.0).astype(BF16)
            m_rk = jnp.where(incl, g[CHUNK:, CHUNK:], 0.0).astype(BF16)
            x0_s[idx] = jnp.dot(l_ak, vhh, preferred_element_type=F32)
            y0_s[idx] = jnp.dot(m_rk, vhh, preferred_element_type=F32)
            mrb_s[idx] = jnp.where(incl, g[CHUNK:, 0:CHUNK], 0.0).astype(BF16)
            tinv_s[idx] = tinv[h].astype(BF16)
            ar_s[idx] = ar[h]
            bke_s[idx] = jnp.concatenate([be[:, hsl[h]], ke[:, hsl[h]]], axis=0).astype(BF16)
        return carry

    lax.fori_loop(0, n_chunks, phase_a, 0)

    def phase_b(c, carry):
        r0 = pl.multiple_of(c * CHUNK, CHUNK)
        rows = pl.ds(r0, CHUNK)
        wc = jnp.exp(cum_s[rows, :][CHUNK - 1:CHUNK, :])
        vh = v_s[rows, :]
        idx = [c * A_HEADS + h for h in heads]
        s0 = [state_ref[h] for h in heads]
        ars = [lax.dot_general(ar_s[idx[h]], s0[h].astype(BF16), _NT,
                               preferred_element_type=F32) for h in heads]
        u = [jnp.dot(tinv_s[idx[h]], (ars[h][0:CHUNK] + x0_s[idx[h]]).astype(BF16),
                     preferred_element_type=F32) for h in heads]
        y = [ars[h][CHUNK:] + y0_s[idx[h]]
             + jnp.dot(mrb_s[idx[h]], u[h].astype(BF16), preferred_element_type=F32)
             for h in heads]
        for h in heads:
            uv = jnp.concatenate([u[h], vh[:, hsl[h]]], axis=0).astype(BF16)
            state_ref[h] = s0[h] * wc[:, hsl[h]] + lax.dot_general(
                uv, bke_s[idx[h]], _TN, preferred_element_type=F32)
        for h in heads:
            mean = jnp.mean(y[h], axis=-1, keepdims=True)
            yc = y[h] - mean
            var = jnp.mean(yc * yc, axis=-1, keepdims=True)
            y_s[rows, hsl[h]] = yc * lax.rsqrt(var + RWKV_GN_EPS)
        return carry

    lax.fori_loop(0, n_chunks, phase_b, 0)

    out = (y_s[...] * gnw_ref[...] + gnb_ref[...] + bonus_s[...]) * g_s[...]
    o_ref[...] = out.astype(o_ref.dtype)


def _rwkv(p, nb, seq, tb, mu, w0, w2, a0, a2, g2, k_k, k_a, r_k, gn_w, gn_b):
    n = p.shape[0]
    nt = seq // tb
    row2 = lambda t: t.reshape(1, -1).astype(F32)
    vec_spec = lambda width: pl.BlockSpec((1, width), lambda b, t: (0, 0))
    full = lambda arr: pl.BlockSpec(arr.shape, lambda b, t: (0, 0))
    w2b, a2b, g2b = w2.astype(BF16), a2.astype(BF16), g2.astype(BF16)
    scr = lambda: pltpu.VMEM((tb, WIDTH), F32)
    nch = (tb // CHUNK) * A_HEADS
    return pl.pallas_call(
        _rwkv_kernel,
        grid=(nb, nt),
        in_specs=[pl.BlockSpec((tb, 3 * WIDTH), lambda b, t: (b * nt + t, 0)),
                  pl.BlockSpec((tb, A_LORA), lambda b, t: (b * nt + t, LORA_OFF // A_LORA)),
                  vec_spec(3 * WIDTH), vec_spec(A_LORA), vec_spec(WIDTH), full(w2b),
                  vec_spec(WIDTH), full(a2b), full(g2b), vec_spec(WIDTH), vec_spec(WIDTH),
                  vec_spec(WIDTH), vec_spec(WIDTH), vec_spec(WIDTH)],
        out_specs=pl.BlockSpec((tb, WIDTH), lambda b, t: (b * nt + t, 0)),
        out_shape=jax.ShapeDtypeStruct((n, WIDTH), BF16),
        scratch_shapes=[pltpu.VMEM((A_HEADS, HEAD_DIM, HEAD_DIM), F32),
                        pltpu.VMEM((1, 3 * WIDTH), F32),
                        pltpu.VMEM((1, A_LORA), F32)] + [scr() for _ in range(10)]
        + [pltpu.VMEM((nch, CHUNK, CHUNK), F32), pltpu.VMEM((nch, CHUNK, CHUNK), F32),
           pltpu.VMEM((nch, CHUNK, CHUNK), BF16), pltpu.VMEM((nch, CHUNK, CHUNK), BF16),
           pltpu.VMEM((nch, 2 * CHUNK, HEAD_DIM), BF16),
           pltpu.VMEM((nch, 2 * CHUNK, HEAD_DIM), BF16)],
        compiler_params=_params(("parallel", "arbitrary")),
        name="rwkv",
    )(p, p, row2(mu[:3 * WIDTH]), row2(mu[3 * WIDTH:]), row2(w0), w2b, row2(a0), a2b, g2b,
      row2(k_k), row2(k_a), row2(r_k), row2(gn_w), row2(gn_b))


def _band_kernel(q_ref, kp_ref, kc_ref, vp_ref, vc_ref, bias_ref, o_ref, k_s, vt_s):
    tq = q_ref.shape[0]
    i = pl.program_id(1)
    pair = 2 * CHUNK
    nkeys = B_BAND + CHUNK
    hsl = [slice(h * HEAD_DIM, (h + 1) * HEAD_DIM) for h in range(B_HEADS)]

    pad_rows = vt_s.shape[1] - HEAD_DIM
    rid = lax.broadcasted_iota(jnp.int32, (pad_rows, 2 * tq), 0)
    ones_rows = jnp.where(rid == 0, 1.0, 0.0).astype(BF16)
    for half, (kr, vr) in enumerate(((kp_ref, vp_ref), (kc_ref, vc_ref))):
        rows = slice(half * tq, (half + 1) * tq)
        vt = vr[...].astype(F32).T
        for h in range(B_HEADS):
            k_s[h, rows, :] = kr[:, hsl[h]]
            vt_s[h, 0:HEAD_DIM, rows] = vt[hsl[h], :].astype(BF16)
    for h in range(B_HEADS):
        vt_s[h, HEAD_DIM:, :] = ones_rows

    q = q_ref[...] * (HEAD_DIM ** -0.5)
    krow = lax.broadcasted_iota(jnp.int32, (nkeys, pair), 0)
    for pi in range(tq // pair):
        off = pi * pair
        valid = jnp.logical_or(i > 0, krow + off >= tq)
        outs = []
        for h in range(B_HEADS):
            kw = k_s[h, off:off + nkeys, :]
            s = lax.dot_general(kw, q[off:off + pair, hsl[h]], _NT,
                                preferred_element_type=F32) + bias_ref[h]
            s = jnp.where(valid, s, NEG_INF)
            m = jnp.max(s, axis=0, keepdims=True)
            e = jnp.exp(s - m).astype(BF16)
            acc = jnp.dot(vt_s[h, :, off:off + nkeys], e, preferred_element_type=F32)
            outs.append(acc[0:HEAD_DIM] / acc[HEAD_DIM:HEAD_DIM + 1])
        o_ref[off:off + pair, :] = jnp.concatenate(outs, axis=0).T.astype(o_ref.dtype)


def _band(p, nb, seq, tq, bias):
    n = p.shape[0]
    nt = seq // tq
    qb, kb, vb = B_OFF // WIDTH, B_OFF // WIDTH + 1, B_OFF // WIDTH + 2
    cur = lambda cb: pl.BlockSpec((tq, WIDTH), lambda b, t: (b * nt + t, cb))
    prv = lambda cb: pl.BlockSpec((tq, WIDTH), lambda b, t: (b * nt + jnp.maximum(t - 1, 0), cb))
    return pl.pallas_call(
        _band_kernel,
        grid=(nb, nt),
        in_specs=[cur(qb), prv(kb), cur(kb), prv(vb), cur(vb),
                  pl.BlockSpec(bias.shape, lambda b, t: (0, 0, 0))],
        out_specs=pl.BlockSpec((tq, WIDTH), lambda b, t: (b * nt + t, 0)),
        out_shape=jax.ShapeDtypeStruct((n, WIDTH), BF16),
        scratch_shapes=[pltpu.VMEM((B_HEADS, 2 * tq, HEAD_DIM), BF16),
                        pltpu.VMEM((B_HEADS, HEAD_DIM + BF16_SUBLANES, 2 * tq), BF16)],
        compiler_params=_params(("parallel", "arbitrary")),
        name="band",
    )(p, p, p, p, p, bias)


def _diff_kernel(q_ref, k_ref, v_ref, bd_ref, bp_ref,
                 lq1_ref, lk1_ref, lq2_ref, lk2_ref, sub_ref, o_ref,
                 k_s, vt_s, m_s, acc_s, *, lam_init):
    tq = q_ref.shape[0]
    seq = k_ref.shape[0]
    hw = 2 * HEAD_DIM
    i = pl.program_id(2)

    @pl.when(i == 0)
    def _():
        for comp in range(2):
            k_s[comp] = k_ref[:, comp * HEAD_DIM:(comp + 1) * HEAD_DIM]
        pad_rows = vt_s.shape[1] - hw
        rid = lax.broadcasted_iota(jnp.int32, (pad_rows, tq), 0)
        ones_row = jnp.where(rid == 0, 1.0, 0.0).astype(BF16)
        for j in range(seq // tq):
            vt = v_ref[j * tq:(j + 1) * tq, :].astype(F32).T.astype(BF16)
            vt_s[j] = jnp.concatenate([vt, ones_row], axis=0)

    q = q_ref[...] * (HEAD_DIM ** -0.5)
    qs = (q[:, 0:HEAD_DIM], q[:, HEAD_DIM:])

    def scores(j, comp, bias):
        r0 = pl.multiple_of(j * tq, tq)
        kc = k_s[comp, pl.ds(r0, tq), :]
        s = lax.dot_general(kc, qs[comp], _NT, preferred_element_type=F32)
        return s if bias is None else s + bias

    def first_tile(j, bias):
        for comp in range(2):
            s = scores(j, comp, bias)
            m = jnp.max(s, axis=0, keepdims=True)
            e = jnp.exp(s - m).astype(BF16)
            m_s[comp] = m
            acc_s[comp] = jnp.dot(vt_s[j], e, preferred_element_type=F32)

    def next_tile(j, bias):
        for comp in range(2):
            s = scores(j, comp, bias)
            m_old = m_s[comp]
            m_new = jnp.maximum(m_old, jnp.max(s, axis=0, keepdims=True))
            alpha = jnp.exp(m_old - m_new)
            e = jnp.exp(s - m_new).astype(BF16)
            m_s[comp] = m_new
            acc_s[comp] = alpha * acc_s[comp] + jnp.dot(vt_s[j], e,
                                                        preferred_element_type=F32)

    first_tile(i, bd_ref[...])

    @pl.when(i >= 1)
    def _():
        next_tile(i - 1, bp_ref[...])

    def far_body(j, carry):
        next_tile(j, None)
        return carry

    lax.fori_loop(0, jnp.maximum(i - 1, 0), far_body, 0)

    lam = (jnp.exp(jnp.sum(lq1_ref[...] * lk1_ref[...], axis=-1, keepdims=True))
           - jnp.exp(jnp.sum(lq2_ref[...] * lk2_ref[...], axis=-1, keepdims=True))
           + lam_init)
    a1, a2 = acc_s[0], acc_s[1]
    out = a1[0:hw] / a1[hw:hw + 1] - lam * (a2[0:hw] / a2[hw:hw + 1])
    out = out * lax.rsqrt(jnp.mean(out * out, axis=0, keepdims=True) + SUBLN_EPS)
    out = out * sub_ref[...] * (1.0 - lam_init)
    o_ref[...] = out.T.astype(o_ref.dtype)


def _diff(p, nb, seq, tq, bias_diag, bias_prev, lq1, lk1, lq2, lk2, subw, lam_init):
    n = p.shape[0]
    nt = seq // tq
    hw = 2 * HEAD_DIM
    vrows = hw + BF16_SUBLANES
    qb, kb, vb = C_OFF // hw, (C_OFF + WIDTH) // hw, (C_OFF + 2 * WIDTH) // hw
    row2 = lambda t: t.reshape(1, -1).astype(F32)
    vec = lambda width: pl.BlockSpec((1, width), lambda b, h, t: (0, 0))
    return pl.pallas_call(
        functools.partial(_diff_kernel, lam_init=lam_init),
        grid=(nb, C_HEADS, nt),
        in_specs=[pl.BlockSpec((tq, hw), lambda b, h, t: (b * nt + t, qb + h)),
                  pl.BlockSpec((seq, hw), lambda b, h, t: (b, kb + h)),
                  pl.BlockSpec((seq, hw), lambda b, h, t: (b, vb + h)),
                  pl.BlockSpec((None, tq, tq), lambda b, h, t: (h, 0, 0)),
                  pl.BlockSpec((None, tq, tq), lambda b, h, t: (h, 0, 0)),
                  vec(HEAD_DIM), vec(HEAD_DIM), vec(HEAD_DIM), vec(HEAD_DIM),
                  pl.BlockSpec((hw, 1), lambda b, h, t: (0, 0))],
        out_specs=pl.BlockSpec((tq, hw), lambda b, h, t: (b * nt + t, h)),
        out_shape=jax.ShapeDtypeStruct((n, WIDTH), BF16),
        scratch_shapes=[pltpu.VMEM((2, seq, HEAD_DIM), BF16),
                        pltpu.VMEM((nt, vrows, tq), BF16),
                        pltpu.VMEM((2, 1, tq), F32),
                        pltpu.VMEM((2, vrows, tq), F32)],
        compiler_params=_params(("parallel", "parallel", "arbitrary")),
        name="diffattn",
    )(p, p, p, bias_diag, bias_prev, row2(lq1), row2(lk1), row2(lq2),
      row2(lk2), subw.reshape(-1, 1).astype(F32))


def _t5_buckets(rel):
    nb = T5_BUCKETS // 2
    max_exact = nb // 2
    ret = (rel > 0).astype(jnp.int32) * nb
    n = jnp.abs(rel)
    nf = jnp.maximum(n, 1).astype(jnp.float32)
    large = max_exact + (jnp.log(nf / max_exact) / math.log(T5_MAX_DIST / max_exact)
                         * (nb - max_exact)).astype(jnp.int32)
    large = jnp.minimum(large, nb - 1)
    return ret + jnp.where(n < max_exact, n, large)


def _toeplitz(vec, rows, cols):
    h, period = vec.shape
    flat = jnp.tile(vec, (1, rows))[:, :rows * (period - 1)]
    return flat.reshape(h, rows, period - 1)[:, :, :cols]


def _t5_tiles(t5_table, tq):
    tab = t5_table.astype(F32)
    period = 2 * tq
    m = jnp.arange(period)
    qk = jnp.where(m < tq, m, m - period)
    far = tab[_t5_buckets(jnp.int32(-2 * tq))]
    vec_d = tab[_t5_buckets(-qk)].T - far[:, None]
    vec_p = tab[_t5_buckets(-tq - qk)].T - far[:, None]
    diag = _toeplitz(vec_d, tq, tq)
    prev = _toeplitz(vec_p, tq, tq)
    qi = jnp.arange(tq)[None, :]
    ki = jnp.arange(tq)[:, None]
    diag = jnp.where((ki // CHUNK) <= (qi // CHUNK), diag, NEG_INF)
    return diag, prev


def _band_bias(rel_bias, keys, queries):
    period = keys + queries
    m = jnp.arange(period)
    qk = jnp.where(m < queries, m, m - period)
    dist = B_LEFT_CHUNKS * CHUNK + qk
    vec = rel_bias[:, jnp.clip(dist, -REL_CLIP, REL_CLIP) + REL_CLIP].astype(F32)
    bias = _toeplitz(vec, keys, queries)
    qi = jnp.arange(queries)[None, :]
    ki = jnp.arange(keys)[:, None]
    in_window = jnp.where(qi < CHUNK, ki < B_BAND, ki >= CHUNK)
    return jnp.where(in_window, bias, NEG_INF)


def _outproj_kernel(ya_ref, yb_ref, yc_ref, pb_ref, pc_ref, ph_ref, pcp_ref, php_ref,
                    cw_ref, w_ref, x_ref, o_ref, *, tiles_per_seq):
    tm = x_ref.shape[0]
    i = pl.program_id(0)
    u = pc_ref[...].astype(F32) * ph_ref[...].astype(F32)
    up = pcp_ref[...].astype(F32) * php_ref[...].astype(F32)
    up = jnp.where(i % tiles_per_seq == 0, 0.0, up)
    row = lax.broadcasted_iota(jnp.int32, (tm, 1), 0)
    s1 = jnp.where(row == 0, up[7:8, :], pltpu.roll(u, 1, axis=0))
    s2 = pltpu.roll(u, 2, axis=0)
    s2 = jnp.where(row == 0, up[6:7, :], jnp.where(row == 1, up[7:8, :], s2))
    cw = cw_ref[...]
    yd = pb_ref[...].astype(F32) * (cw[0:1, :] * s2 + cw[1:2, :] * s1 + cw[2:3, :] * u)
    acc = jnp.dot(ya_ref[...], w_ref[0:WIDTH, :], preferred_element_type=F32)
    acc += jnp.dot(yb_ref[...], w_ref[WIDTH:2 * WIDTH, :], preferred_element_type=F32)
    acc += jnp.dot(yc_ref[...], w_ref[2 * WIDTH:3 * WIDTH, :], preferred_element_type=F32)
    acc += jnp.dot(yd.astype(BF16), w_ref[3 * WIDTH:, :], preferred_element_type=F32)
    o_ref[...] = x_ref[...] + acc


def _outproj(ya, yb, yc, p, conv_w, w_bf16, x2d, seq, tm):
    n, d = x2d.shape
    db = D_OFF // WIDTH
    r8 = tm // 8
    ycur = pl.BlockSpec((tm, WIDTH), lambda i: (i, 0))
    pcur = lambda cb: pl.BlockSpec((tm, WIDTH), lambda i: (i, cb))
    pprev = lambda cb: pl.BlockSpec((8, WIDTH), lambda i: (jnp.maximum(i * r8 - 1, 0), cb))
    return pl.pallas_call(
        functools.partial(_outproj_kernel, tiles_per_seq=seq // tm),
        grid=(n // tm,),
        in_specs=[ycur, ycur, ycur, pcur(db), pcur(db + 1), pcur(db + 2),
                  pprev(db + 1), pprev(db + 2),
                  pl.BlockSpec(conv_w.shape, lambda i: (0, 0)),
                  pl.BlockSpec(w_bf16.shape, lambda i: (0, 0)),
                  pl.BlockSpec((tm, d), lambda i: (i, 0))],
        out_specs=pl.BlockSpec((tm, d), lambda i: (i, 0)),
        out_shape=jax.ShapeDtypeStruct((n, d), F32),
        compiler_params=_params(("parallel",)),
        name="outproj",
    )(ya, yb, yc, p, p, p, p, p, conv_w.astype(F32), w_bf16, x2d)


def _router_kernel(x_ref, g_ref, wh_ref, wl_ref, b_ref, xn_ref, comb_ref):
    x = x_ref[...]
    ms = jnp.mean(x * x, axis=-1, keepdims=True)
    xn = x * lax.rsqrt(ms + RMS_EPS) * g_ref[...]
    xh = xn.astype(BF16)
    xn_ref[...] = xh
    xl = (xn - xh.astype(F32)).astype(BF16)
    logits = (jnp.dot(xh, wh_ref[...], preferred_element_type=F32)
              + jnp.dot(xh, wl_ref[...], preferred_element_type=F32)
              + jnp.dot(xl, wh_ref[...], preferred_element_type=F32)) + b_ref[...]
    tm = x.shape[0]
    lane_i = lax.broadcasted_iota(jnp.int32, (tm, LANES), 1)
    lane = lane_i.astype(F32)
    lane_grp = lax.shift_right_logical(lane_i, 3).astype(F32)
    is_g = (lane_i >= N_EXPERTS) & (lane_i < N_EXPERTS + N_GROUPS)
    gl = jnp.where(is_g, logits, NEG_INF)
    gmax = jnp.max(gl, axis=-1, keepdims=True)
    gsum = jnp.sum(jnp.where(is_g, jnp.exp(gl - gmax), 0.0), axis=-1, keepdims=True)
    g_w = 1.0 / gsum
    g_idx = jnp.min(jnp.where(is_g & (gl == gmax), lane, float(LANES)), axis=-1,
                    keepdims=True) - float(N_EXPERTS)
    in_grp = (lane_i < N_EXPERTS) & (lane_grp == g_idx)
    el = jnp.where(in_grp, logits, NEG_INF)
    emax = jnp.max(el, axis=-1, keepdims=True)
    ee = jnp.where(in_grp, jnp.exp(el - emax), 0.0)
    ep = ee / jnp.sum(ee, axis=-1, keepdims=True)
    p1 = jnp.max(ep, axis=-1, keepdims=True)
    i1 = jnp.min(jnp.where(in_grp & (ep == p1), lane, float(LANES)), axis=-1, keepdims=True)
    rest = in_grp & (lane != i1)
    ep2 = jnp.where(rest, ep, -1.0)
    p2 = jnp.max(ep2, axis=-1, keepdims=True)
    i2 = jnp.min(jnp.where(rest & (ep2 == p2), lane, float(LANES)), axis=-1, keepdims=True)
    tot = p1 + p2
    comb = jnp.where(lane == i1, p1 / tot, 0.0) + jnp.where(lane == i2, p2 / tot, 0.0)
    comb_ref[...] = comb * g_w


def _router(x2d, gain, wr_hi, wr_lo, br, tm):
    n, d = x2d.shape
    return pl.pallas_call(
        _router_kernel,
        grid=(n // tm,),
        in_specs=[pl.BlockSpec((tm, d), lambda i: (i, 0)),
                  pl.BlockSpec((1, d), lambda i: (0, 0)),
                  pl.BlockSpec((d, LANES), lambda i: (0, 0)),
                  pl.BlockSpec((d, LANES), lambda i: (0, 0)),
                  pl.BlockSpec((1, LANES), lambda i: (0, 0))],
        out_specs=[pl.BlockSpec((tm, d), lambda i: (i, 0)),
                   pl.BlockSpec((tm, LANES), lambda i: (i, 0))],
        out_shape=[jax.ShapeDtypeStruct((n, d), BF16),
                   jax.ShapeDtypeStruct((n, LANES), F32)],
        compiler_params=_params(("parallel",)),
        name="router",
    )(x2d, gain, wr_hi, wr_lo, br)


def _experts_kernel(x_ref, xn_ref, comb_ref, wg_ref, wu_ref, wd_ref, gf_ref, o_ref,
                    acc_ref, *, final_norm):
    e = pl.program_id(1)

    @pl.when(e == 0)
    def _():
        acc_ref[...] = jnp.zeros_like(acc_ref)

    xn = xn_ref[...]
    hg = jnp.dot(xn, wg_ref[...], preferred_element_type=F32)
    hu = jnp.dot(xn, wu_ref[...], preferred_element_type=F32)
    lane = lax.broadcasted_iota(jnp.int32, comb_ref.shape, 1)
    cw = jnp.sum(jnp.where(lane == e, comb_ref[...], 0.0), axis=-1, keepdims=True)
    h = hg * _sigmoid(hg) * hu * cw
    acc_ref[...] += jnp.dot(h.astype(BF16), wd_ref[...], preferred_element_type=F32)

    @pl.when(e == pl.num_programs(1) - 1)
    def _():
        y = x_ref[...] + acc_ref[...]
        if final_norm:
            ms = jnp.mean(y * y, axis=-1, keepdims=True)
            y = y * lax.rsqrt(ms + RMS_EPS) * gf_ref[...]
        o_ref[...] = y


def _experts(x2d, xn, comb, wg, wu, wd, gain_final, final_norm, tm):
    n, d = x2d.shape
    ne, _, hid = wg.shape
    return pl.pallas_call(
        functools.partial(_experts_kernel, final_norm=final_norm),
        grid=(n // tm, ne),
        in_specs=[pl.BlockSpec((tm, d), lambda i, e: (i, 0)),
                  pl.BlockSpec((tm, d), lambda i, e: (i, 0)),
                  pl.BlockSpec((tm, LANES), lambda i, e: (i, 0)),
                  pl.BlockSpec((None, d, hid), lambda i, e: (e, 0, 0)),
                  pl.BlockSpec((None, d, hid), lambda i, e: (e, 0, 0)),
                  pl.BlockSpec((None, hid, d), lambda i, e: (e, 0, 0)),
                  pl.BlockSpec((1, d), lambda i, e: (0, 0))],
        out_specs=pl.BlockSpec((tm, d), lambda i, e: (i, 0)),
        out_shape=jax.ShapeDtypeStruct((n, d), F32),
        scratch_shapes=[pltpu.VMEM((tm, d), F32)],
        compiler_params=_params(("parallel", "arbitrary")),
        name="experts",
    )(x2d, xn, comb, wg, wu, wd, gain_final)


def _pick(total, pref):
    t = min(pref, total)
    while total % t:
        t //= 2
    return t


def kernel(x, w_in, w_out, norm_mix, norm_ffn, norm_final, rwkv_mu, rwkv_w0, rwkv_w2, rwkv_a0, rwkv_a2, rwkv_g2, rwkv_k_k, rwkv_k_a, rwkv_r_k, rwkv_gn_w, rwkv_gn_b, band_rel_bias, t5_rel_bias, diff_lambda_q1, diff_lambda_k1, diff_lambda_q2, diff_lambda_k2, diff_subln_w, conv_w, router_group_w, router_group_b, router_expert_w, router_expert_b, expert_w_gate, expert_w_up, expert_w_down):
    nb, seq, d = x.shape
    n = nb * seq
    depth = w_in.shape[0]
    x2d = x.reshape(n, d)

    tm_proj = _pick(n, 1024)
    tm_moe = _pick(n, 512)
    t_rwkv = _pick(seq, 256)
    t_band = _pick(seq, 512)
    t_diff = _pick(seq, 512)
    row2 = lambda t: t.reshape(1, -1).astype(F32)

    t5_diag, t5_prev = _t5_tiles(t5_rel_bias, t_diff)

    for l in range(depth):
        wl = w_in[l]
        w_perm = jnp.concatenate([wl[:, :3 * WIDTH], wl[:, A_COLS:], wl[:, 3 * WIDTH:A_COLS]],
                                 axis=1).astype(BF16)
        p = _inproj(x2d, row2(norm_mix[l]), w_perm, tm_proj, 1280)

        ya = _rwkv(p, nb, seq, t_rwkv, rwkv_mu[l], rwkv_w0[l], rwkv_w2[l], rwkv_a0[l],
                   rwkv_a2[l], rwkv_g2[l], rwkv_k_k[l], rwkv_k_a[l], rwkv_r_k[l],
                   rwkv_gn_w[l], rwkv_gn_b[l])
        yb = _band(p, nb, seq, t_band,
                   _band_bias(band_rel_bias[l], B_BAND + CHUNK, 2 * CHUNK))
        lam_init = 0.8 - 0.6 * math.exp(-0.3 * l)
        yc = _diff(p, nb, seq, t_diff, t5_diag, t5_prev,
                   diff_lambda_q1[l], diff_lambda_k1[l], diff_lambda_q2[l],
                   diff_lambda_k2[l], diff_subln_w[l], lam_init)
        x2d = _outproj(ya, yb, yc, p, conv_w[l], w_out[l].astype(BF16), x2d, seq, tm_moe)

        wr = jnp.concatenate([router_expert_w[l], router_group_w[l]], axis=1).astype(F32)
        wr = jnp.pad(wr, ((0, 0), (0, LANES - wr.shape[1])))
        wr_hi = wr.astype(BF16)
        wr_lo = (wr - wr_hi.astype(F32)).astype(BF16)
        br = jnp.concatenate([router_expert_b[l], router_group_b[l]]).astype(F32)
        br = jnp.pad(br, (0, LANES - br.shape[0])).reshape(1, LANES)
        xn, comb = _router(x2d, row2(norm_ffn[l]), wr_hi, wr_lo, br, tm_moe)
        x2d = _experts(x2d, xn, comb, expert_w_gate[l].astype(BF16),
                       expert_w_up[l].astype(BF16), expert_w_down[l].astype(BF16),
                       row2(norm_final), l == depth - 1, tm_moe)
    return x2d.reshape(nb, seq, d)
```

```python
import functools
import math

import jax
import jax.numpy as jnp
from jax import lax
from jax.experimental import pallas as pl
from jax.experimental.pallas import tpu as pltpu

F32 = jnp.float32
BF16 = jnp.bfloat16

DEPTH = 2
CHUNK = 64
CHUNK_SHIFT = 6
HEAD_DIM = 64
A_HEADS = 8
B_HEADS = 8
C_HEADS = 4
WIDTH = 512
A_DECAY_LORA = 64
A_ICLR_LORA = 64
A_GATE_LORA = 128
A_LORA = A_DECAY_LORA + A_ICLR_LORA + A_GATE_LORA
A_COLS = 3 * WIDTH + A_LORA
B_OFF = 3 * WIDTH
C_OFF = B_OFF + 3 * WIDTH
D_OFF = C_OFF + 3 * WIDTH
LORA_OFF = D_OFF + 3 * WIDTH
IN_COLS = LORA_OFF + A_LORA
B_LEFT_CHUNKS = 8
B_BAND = (B_LEFT_CHUNKS + 1) * CHUNK
REL_CLIP = 128
T5_BUCKETS = 32
T5_MAX_DIST = 128
N_GROUPS = 4
EXPERTS_PER_GROUP = 8
N_EXPERTS = 32
RMS_EPS = 1e-6
RWKV_GN_EPS = 64e-5
SUBLN_EPS = 1e-5
NEG_INF = -1e30

LANES = 128
BF16_SUBLANES = 16
IDX_TILE = 1024

ROUTE_ID1, ROUTE_ID2, ROUTE_W1, ROUTE_W2 = 0, 1, 2, 3
VMEM_LIMIT_BYTES = 56 * 1024 * 1024

_NT = (((1,), (1,)), ((), ()))
_TN = (((0,), (0,)), ((), ()))


def _params(semantics):
    return pltpu.CompilerParams(dimension_semantics=semantics,
                                vmem_limit_bytes=VMEM_LIMIT_BYTES)


def _bdot(a, b):
    return jnp.dot(a.astype(BF16), b.astype(BF16), preferred_element_type=F32)


def _sigmoid(x):
    return 1.0 / (1.0 + jnp.exp(-x))


def _inproj_kernel(x_ref, g_ref, w_ref, o_ref, xn_ref):
    @pl.when(pl.program_id(1) == 0)
    def _():
        x = x_ref[...]
        ms = jnp.mean(x * x, axis=-1, keepdims=True)
        xn_ref[...] = (x * lax.rsqrt(ms + RMS_EPS) * g_ref[...]).astype(BF16)

    o_ref[...] = jnp.dot(xn_ref[...], w_ref[...],
                         preferred_element_type=F32).astype(o_ref.dtype)


def _inproj(x2d, gain, w_bf16, tm, tn):
    n, d = x2d.shape
    cols = w_bf16.shape[1]
    return pl.pallas_call(
        _inproj_kernel,
        grid=(n // tm, cols // tn),
        in_specs=[pl.BlockSpec((tm, d), lambda i, j: (i, 0)),
                  pl.BlockSpec((1, d), lambda i, j: (0, 0)),
                  pl.BlockSpec((d, tn), lambda i, j: (0, j))],
        out_specs=pl.BlockSpec((tm, tn), lambda i, j: (i, j)),
        out_shape=jax.ShapeDtypeStruct((n, cols), BF16),
        scratch_shapes=[pltpu.VMEM((tm, d), BF16)],
        compiler_params=_params(("parallel", "arbitrary")),
        name="inproj",
    )(x2d, gain, w_bf16)


def _rwkv_kernel(p_ref, pl_ref, mu_ref, mul_ref, w0_ref, w2_ref, a0_ref, a2_ref, g2_ref,
                 kk_ref, ka_ref, rk_ref, gnw_ref, gnb_ref,
                 o_ref,
                 state_ref, prev_ref, prevl_ref, r_s, k_s, v_s, na_s, b_s, ld_s, cum_s,
                 y_s, g_s, bonus_s, x0_s, y0_s, mrb_s, tinv_s, ar_s, bke_s):
    tb = p_ref.shape[0]
    n_chunks = tb // CHUNK

    @pl.when(pl.program_id(1) == 0)
    def _():
        state_ref[...] = jnp.zeros_like(state_ref)
        prev_ref[...] = jnp.zeros_like(prev_ref)
        prevl_ref[...] = jnp.zeros_like(prevl_ref)

    row = lax.broadcasted_iota(jnp.int32, (tb, 1), 0)

    def token_shift(src_ref, last_ref, m_ref):
        pa = src_ref[...].astype(F32)
        shifted = jnp.where(row == 0, last_ref[...], pltpu.roll(pa, 1, axis=0))
        last_ref[...] = pa[tb - 1:tb, :]
        return pa + (shifted - pa) * m_ref[...]

    ps = token_shift(p_ref, prev_ref, mu_ref)
    lora = token_shift(pl_ref, prevl_ref, mul_ref)
    r = ps[:, 0:WIDTH]
    k = ps[:, WIDTH:2 * WIDTH]
    v = ps[:, 2 * WIDTH:3 * WIDTH]
    w_lo = lora[:, 0:A_DECAY_LORA]
    a_lo = lora[:, A_DECAY_LORA:A_DECAY_LORA + A_ICLR_LORA]
    g_lo = lora[:, A_DECAY_LORA + A_ICLR_LORA:A_LORA]

    z = -(w0_ref[...] + _bdot(jnp.tanh(w_lo), w2_ref[...]))
    softplus = jnp.maximum(z, 0.0) + jnp.log(1.0 + jnp.exp(-jnp.abs(z)))
    ld = -jnp.exp(-softplus - 0.5)
    a = _sigmoid(a0_ref[...] + _bdot(a_lo, a2_ref[...]))
    g_s[...] = _bdot(_sigmoid(g_lo), g2_ref[...])

    kk = k * kk_ref[...]
    kmod = k * (1.0 + (a - 1.0) * ka_ref[...])
    rkr = r * kmod * rk_ref[...]
    for h in range(A_HEADS):
        sl = slice(h * HEAD_DIM, (h + 1) * HEAD_DIM)
        kkh = kk[:, sl]
        nrm = jnp.sqrt(jnp.sum(kkh * kkh, axis=-1, keepdims=True))
        kkn = kkh / jnp.maximum(nrm, 1e-12)
        na_s[:, sl] = -kkn
        b_s[:, sl] = kkn * a[:, sl]
        bonus_s[:, sl] = jnp.sum(rkr[:, sl], axis=-1, keepdims=True) * v[:, sl]
    r_s[...] = r
    k_s[...] = kmod
    v_s[...] = v
    ld_s[...] = ld

    ri = lax.broadcasted_iota(jnp.int32, (tb, tb), 0)
    ci = lax.broadcasted_iota(jnp.int32, (tb, tb), 1)
    same_chunk = (lax.shift_right_logical(ri, CHUNK_SHIFT)
                  == lax.shift_right_logical(ci, CHUNK_SHIFT))
    tri = jnp.where((ri >= ci) & same_chunk, 1.0, 0.0).astype(BF16)
    ld_hi = ld.astype(BF16)
    ld_lo = (ld - ld_hi.astype(F32)).astype(BF16)
    cum_s[...] = (jnp.dot(tri, ld_hi, preferred_element_type=F32)
                  + jnp.dot(tri, ld_lo, preferred_element_type=F32))

    rc = lax.broadcasted_iota(jnp.int32, (CHUNK, CHUNK), 0)
    cc = lax.broadcasted_iota(jnp.int32, (CHUNK, CHUNK), 1)
    strict = rc > cc
    incl = rc >= cc
    eye = jnp.where(rc == cc, 1.0, 0.0)

    heads = range(A_HEADS)
    hsl = [slice(h * HEAD_DIM, (h + 1) * HEAD_DIM) for h in heads]

    def phase_a(c, carry):
        r0 = pl.multiple_of(c * CHUNK, CHUNK)
        rows = pl.ds(r0, CHUNK)
        cum = cum_s[rows, :]
        cum_last = cum[CHUNK - 1:CHUNK, :]
        w_inv = jnp.exp(-cum)
        w_end = jnp.exp(cum_last - cum)
        at = na_s[rows, :] * jnp.exp(cum - ld_s[rows, :])
        rt = r_s[rows, :] * jnp.exp(cum)
        bh = b_s[rows, :]
        kh = k_s[rows, :]
        vh = v_s[rows, :].astype(BF16)
        bt, kt = bh * w_inv, kh * w_inv
        be, ke = bh * w_end, kh * w_end
        ar = [jnp.concatenate([at[:, s], rt[:, s]], axis=0).astype(BF16) for s in hsl]
        bk = [jnp.concatenate([bt[:, s], kt[:, s]], axis=0).astype(BF16) for s in hsl]
        gram = [lax.dot_general(ar[h], bk[h], _NT, preferred_element_type=F32) for h in heads]
        l_ab = [jnp.where(strict, g[0:CHUNK, 0:CHUNK], 0.0) for g in gram]
        tinv = [eye + m for m in l_ab]
        mpow = l_ab
        for _ in range(5):
            mpow = [_bdot(m, m) for m in mpow]
            tinv = [t + _bdot(m, t) for m, t in zip(mpow, tinv)]
        for h in heads:
            idx = c * A_HEADS + h
            g = gram[h]
            vhh = vh[:, hsl[h]]
            l_ak = jnp.where(strict, g[0:CHUNK, CHUNK:], 0.0).astype(BF16)
            m_rk = jnp.where(incl, g[CHUNK:, CHUNK:], 0.0).astype(BF16)
            x0_s[idx] = jnp.dot(l_ak, vhh, preferred_element_type=F32)
            y0_s[idx] = jnp.dot(m_rk, vhh, preferred_element_type=F32)
            mrb_s[idx] = jnp.where(incl, g[CHUNK:, 0:CHUNK], 0.0).astype(BF16)
            tinv_s[idx] = tinv[h].astype(BF16)
            ar_s[idx] = ar[h]
            bke_s[idx] = jnp.concatenate([be[:, hsl[h]], ke[:, hsl[h]]], axis=0).astype(BF16)
        return carry

    lax.fori_loop(0, n_chunks, phase_a, 0)

    def phase_b(c, carry):
        r0 = pl.multiple_of(c * CHUNK, CHUNK)
        rows = pl.ds(r0, CHUNK)
        wc = jnp.exp(cum_s[rows, :][CHUNK - 1:CHUNK, :])
        vh = v_s[rows, :]
        idx = [c * A_HEADS + h for h in heads]
        s0 = [state_ref[h] for h in heads]
        ars = [lax.dot_general(ar_s[idx[h]], s0[h].astype(BF16), _NT,
                               preferred_element_type=F32) for h in heads]
        u = [jnp.dot(tinv_s[idx[h]], (ars[h][0:CHUNK] + x0_s[idx[h]]).astype(BF16),
                     preferred_element_type=F32) for h in heads]
        y = [ars[h][CHUNK:] + y0_s[idx[h]]
             + jnp.dot(mrb_s[idx[h]], u[h].astype(BF16), preferred_element_type=F32)
             for h in heads]
        for h in heads:
            uv = jnp.concatenate([u[h], vh[:, hsl[h]]], axis=0).astype(BF16)
            state_ref[h] = s0[h] * wc[:, hsl[h]] + lax.dot_general(
                uv, bke_s[idx[h]], _TN, preferred_element_type=F32)
        for h in heads:
            mean = jnp.mean(y[h], axis=-1, keepdims=True)
            yc = y[h] - mean
            var = jnp.mean(yc * yc, axis=-1, keepdims=True)
            y_s[rows, hsl[h]] = yc * lax.rsqrt(var + RWKV_GN_EPS)
        return carry

    lax.fori_loop(0, n_chunks, phase_b, 0)

    out = (y_s[...] * gnw_ref[...] + gnb_ref[...] + bonus_s[...]) * g_s[...]
    o_ref[...] = out.astype(o_ref.dtype)


def _rwkv(p, nb, seq, tb, mu, w0, w2, a0, a2, g2, k_k, k_a, r_k, gn_w, gn_b):
    n = p.shape[0]
    nt = seq // tb
    row2 = lambda t: t.reshape(1, -1).astype(F32)
    vec_spec = lambda width: pl.BlockSpec((1, width), lambda b, t: (0, 0))
    full = lambda arr: pl.BlockSpec(arr.shape, lambda b, t: (0, 0))
    w2b, a2b, g2b = w2.astype(BF16), a2.astype(BF16), g2.astype(BF16)
    scr = lambda: pltpu.VMEM((tb, WIDTH), F32)
    nch = (tb // CHUNK) * A_HEADS
    return pl.pallas_call(
        _rwkv_kernel,
        grid=(nb, nt),
        in_specs=[pl.BlockSpec((tb, 3 * WIDTH), lambda b, t: (b * nt + t, 0)),
                  pl.BlockSpec((tb, A_LORA), lambda b, t: (b * nt + t, LORA_OFF // A_LORA)),
                  vec_spec(3 * WIDTH), vec_spec(A_LORA), vec_spec(WIDTH), full(w2b),
                  vec_spec(WIDTH), full(a2b), full(g2b), vec_spec(WIDTH), vec_spec(WIDTH),
                  vec_spec(WIDTH), vec_spec(WIDTH), vec_spec(WIDTH)],
        out_specs=pl.BlockSpec((tb, WIDTH), lambda b, t: (b * nt + t, 0)),
        out_shape=jax.ShapeDtypeStruct((n, WIDTH), BF16),
        scratch_shapes=[pltpu.VMEM((A_HEADS, HEAD_DIM, HEAD_DIM), F32),
                        pltpu.VMEM((1, 3 * WIDTH), F32),
                        pltpu.VMEM((1, A_LORA), F32)] + [scr() for _ in range(10)]
        + [pltpu.VMEM((nch, CHUNK, CHUNK), F32), pltpu.VMEM((nch, CHUNK, CHUNK), F32),
           pltpu.VMEM((nch, CHUNK, CHUNK), BF16), pltpu.VMEM((nch, CHUNK, CHUNK), BF16),
           pltpu.VMEM((nch, 2 * CHUNK, HEAD_DIM), BF16),
           pltpu.VMEM((nch, 2 * CHUNK, HEAD_DIM), BF16)],
        compiler_params=_params(("parallel", "arbitrary")),
        name="rwkv",
    )(p, p, row2(mu[:3 * WIDTH]), row2(mu[3 * WIDTH:]), row2(w0), w2b, row2(a0), a2b, g2b,
      row2(k_k), row2(k_a), row2(r_k), row2(gn_w), row2(gn_b))


def _band_kernel(q_ref, kp_ref, kc_ref, vp_ref, vc_ref, bias_ref, o_ref, k_s, vt_s):
    tq = q_ref.shape[0]
    i = pl.program_id(1)
    pair = 2 * CHUNK
    nkeys = B_BAND + CHUNK
    hsl = [slice(h * HEAD_DIM, (h + 1) * HEAD_DIM) for h in range(B_HEADS)]

    pad_rows = vt_s.shape[1] - HEAD_DIM
    rid = lax.broadcasted_iota(jnp.int32, (pad_rows, 2 * tq), 0)
    ones_rows = jnp.where(rid == 0, 1.0, 0.0).astype(BF16)
    for half, (kr, vr) in enumerate(((kp_ref, vp_ref), (kc_ref, vc_ref))):
        rows = slice(half * tq, (half + 1) * tq)
        vt = vr[...].astype(F32).T
        for h in range(B_HEADS):
            k_s[h, rows, :] = kr[:, hsl[h]]
            vt_s[h, 0:HEAD_DIM, rows] = vt[hsl[h], :].astype(BF16)
    for h in range(B_HEADS):
        vt_s[h, HEAD_DIM:, :] = ones_rows

    q = q_ref[...] * (HEAD_DIM ** -0.5)
    krow = lax.broadcasted_iota(jnp.int32, (nkeys, pair), 0)
    for pi in range(tq // pair):
        off = pi * pair
        valid = jnp.logical_or(i > 0, krow + off >= tq)
        outs = []
        for h in range(B_HEADS):
            kw = k_s[h, off:off + nkeys, :]
            s = lax.dot_general(kw, q[off:off + pair, hsl[h]], _NT,
                                preferred_element_type=F32) + bias_ref[h]
            s = jnp.where(valid, s, NEG_INF)
            m = jnp.max(s, axis=0, keepdims=True)
            e = jnp.exp(s - m).astype(BF16)
            acc = jnp.dot(vt_s[h, :, off:off + nkeys], e, preferred_element_type=F32)
            outs.append(acc[0:HEAD_DIM] / acc[HEAD_DIM:HEAD_DIM + 1])
        o_ref[off:off + pair, :] = jnp.concatenate(outs, axis=0).T.astype(o_ref.dtype)


def _band(p, nb, seq, tq, bias):
    n = p.shape[0]
    nt = seq // tq
    qb, kb, vb = B_OFF // WIDTH, B_OFF // WIDTH + 1, B_OFF // WIDTH + 2
    cur = lambda cb: pl.BlockSpec((tq, WIDTH), lambda b, t: (b * nt + t, cb))
    prv = lambda cb: pl.BlockSpec((tq, WIDTH), lambda b, t: (b * nt + jnp.maximum(t - 1, 0), cb))
    return pl.pallas_call(
        _band_kernel,
        grid=(nb, nt),
        in_specs=[cur(qb), prv(kb), cur(kb), prv(vb), cur(vb),
                  pl.BlockSpec(bias.shape, lambda b, t: (0, 0, 0))],
        out_specs=pl.BlockSpec((tq, WIDTH), lambda b, t: (b * nt + t, 0)),
        out_shape=jax.ShapeDtypeStruct((n, WIDTH), BF16),
        scratch_shapes=[pltpu.VMEM((B_HEADS, 2 * tq, HEAD_DIM), BF16),
                        pltpu.VMEM((B_HEADS, HEAD_DIM + BF16_SUBLANES, 2 * tq), BF16)],
        compiler_params=_params(("parallel", "arbitrary")),
        name="band",
    )(p, p, p, p, p, bias)


def _diff_kernel(q_ref, k_ref, v_ref, bd_ref, bp_ref,
                 lq1_ref, lk1_ref, lq2_ref, lk2_ref, sub_ref, o_ref,
                 k_s, vt_s, m_s, acc_s, *, lam_init):
    tq = q_ref.shape[0]
    seq = k_ref.shape[0]
    hw = 2 * HEAD_DIM
    i = pl.program_id(2)

    @pl.when(i == 0)
    def _():
        for comp in range(2):
            k_s[comp] = k_ref[:, comp * HEAD_DIM:(comp + 1) * HEAD_DIM]
        pad_rows = vt_s.shape[1] - hw
        rid = lax.broadcasted_iota(jnp.int32, (pad_rows, tq), 0)
        ones_row = jnp.where(rid == 0, 1.0, 0.0).astype(BF16)
        for j in range(seq // tq):
            vt = v_ref[j * tq:(j + 1) * tq, :].astype(F32).T.astype(BF16)
            vt_s[j] = jnp.concatenate([vt, ones_row], axis=0)

    q = q_ref[...] * (HEAD_DIM ** -0.5)
    qs = (q[:, 0:HEAD_DIM], q[:, HEAD_DIM:])

    def scores(j, comp, bias):
        r0 = pl.multiple_of(j * tq, tq)
        kc = k_s[comp, pl.ds(r0, tq), :]
        s = lax.dot_general(kc, qs[comp], _NT, preferred_element_type=F32)
        return s if bias is None else s + bias

    def first_tile(j, bias):
        for comp in range(2):
            s = scores(j, comp, bias)
            m = jnp.max(s, axis=0, keepdims=True)
            e = jnp.exp(s - m).astype(BF16)
            m_s[comp] = m
            acc_s[comp] = jnp.dot(vt_s[j], e, preferred_element_type=F32)

    def next_tile(j, bias):
        for comp in range(2):
            s = scores(j, comp, bias)
            m_old = m_s[comp]
            m_new = jnp.maximum(m_old, jnp.max(s, axis=0, keepdims=True))
            alpha = jnp.exp(m_old - m_new)
            e = jnp.exp(s - m_new).astype(BF16)
            m_s[comp] = m_new
            acc_s[comp] = alpha * acc_s[comp] + jnp.dot(vt_s[j], e,
                                                        preferred_element_type=F32)

    first_tile(i, bd_ref[...])

    @pl.when(i >= 1)
    def _():
        next_tile(i - 1, bp_ref[...])

    def far_body(j, carry):
        next_tile(j, None)
        return carry

    lax.fori_loop(0, jnp.maximum(i - 1, 0), far_body, 0)

    lam = (jnp.exp(jnp.sum(lq1_ref[...] * lk1_ref[...], axis=-1, keepdims=True))
           - jnp.exp(jnp.sum(lq2_ref[...] * lk2_ref[...], axis=-1, keepdims=True))
           + lam_init)
    a1, a2 = acc_s[0], acc_s[1]
    out = a1[0:hw] / a1[hw:hw + 1] - lam * (a2[0:hw] / a2[hw:hw + 1])
    out = out * lax.rsqrt(jnp.mean(out * out, axis=0, keepdims=True) + SUBLN_EPS)
    out = out * sub_ref[...] * (1.0 - lam_init)
    o_ref[...] = out.T.astype(o_ref.dtype)


def _diff(p, nb, seq, tq, bias_diag, bias_prev, lq1, lk1, lq2, lk2, subw, lam_init):
    n = p.shape[0]
    nt = seq // tq
    hw = 2 * HEAD_DIM
    vrows = hw + BF16_SUBLANES
    qb, kb, vb = C_OFF // hw, (C_OFF + WIDTH) // hw, (C_OFF + 2 * WIDTH) // hw
    row2 = lambda t: t.reshape(1, -1).astype(F32)
    vec = lambda width: pl.BlockSpec((1, width), lambda b, h, t: (0, 0))
    return pl.pallas_call(
        functools.partial(_diff_kernel, lam_init=lam_init),
        grid=(nb, C_HEADS, nt),
        in_specs=[pl.BlockSpec((tq, hw), lambda b, h, t: (b * nt + t, qb + h)),
                  pl.BlockSpec((seq, hw), lambda b, h, t: (b, kb + h)),
                  pl.BlockSpec((seq, hw), lambda b, h, t: (b, vb + h)),
                  pl.BlockSpec((None, tq, tq), lambda b, h, t: (h, 0, 0)),
                  pl.BlockSpec((None, tq, tq), lambda b, h, t: (h, 0, 0)),
                  vec(HEAD_DIM), vec(HEAD_DIM), vec(HEAD_DIM), vec(HEAD_DIM),
                  pl.BlockSpec((hw, 1), lambda b, h, t: (0, 0))],
        out_specs=pl.BlockSpec((tq, hw), lambda b, h, t: (b * nt + t, h)),
        out_shape=jax.ShapeDtypeStruct((n, WIDTH), BF16),
        scratch_shapes=[pltpu.VMEM((2, seq, HEAD_DIM), BF16),
                        pltpu.VMEM((nt, vrows, tq), BF16),
                        pltpu.VMEM((2, 1, tq), F32),
                        pltpu.VMEM((2, vrows, tq), F32)],
        compiler_params=_params(("parallel", "parallel", "arbitrary")),
        name="diffattn",
    )(p, p, p, bias_diag, bias_prev, row2(lq1), row2(lk1), row2(lq2),
      row2(lk2), subw.reshape(-1, 1).astype(F32))


def _t5_buckets(rel):
    nb = T5_BUCKETS // 2
    max_exact = nb // 2
    ret = (rel > 0).astype(jnp.int32) * nb
    n = jnp.abs(rel)
    nf = jnp.maximum(n, 1).astype(jnp.float32)
    large = max_exact + (jnp.log(nf / max_exact) / math.log(T5_MAX_DIST / max_exact)
                         * (nb - max_exact)).astype(jnp.int32)
    large = jnp.minimum(large, nb - 1)
    return ret + jnp.where(n < max_exact, n, large)


def _toeplitz(vec, rows, cols):
    h, period = vec.shape
    flat = jnp.tile(vec, (1, rows))[:, :rows * (period - 1)]
    return flat.reshape(h, rows, period - 1)[:, :, :cols]


def _t5_tiles(t5_table, tq):
    tab = t5_table.astype(F32)
    period = 2 * tq
    m = jnp.arange(period)
    qk = jnp.where(m < tq, m, m - period)
    far = tab[_t5_buckets(jnp.int32(-2 * tq))]
    vec_d = tab[_t5_buckets(-qk)].T - far[:, None]
    vec_p = tab[_t5_buckets(-tq - qk)].T - far[:, None]
    diag = _toeplitz(vec_d, tq, tq)
    prev = _toeplitz(vec_p, tq, tq)
    qi = jnp.arange(tq)[None, :]
    ki = jnp.arange(tq)[:, None]
    diag = jnp.where((ki // CHUNK) <= (qi // CHUNK), diag, NEG_INF)
    return diag, prev


def _band_bias(rel_bias, keys, queries):
    period = keys + queries
    m = jnp.arange(period)
    qk = jnp.where(m < queries, m, m - period)
    dist = B_LEFT_CHUNKS * CHUNK + qk
    vec = rel_bias[:, jnp.clip(dist, -REL_CLIP, REL_CLIP) + REL_CLIP].astype(F32)
    bias = _toeplitz(vec, keys, queries)
    qi = jnp.arange(queries)[None, :]
    ki = jnp.arange(keys)[:, None]
    in_window = jnp.where(qi < CHUNK, ki < B_BAND, ki >= CHUNK)
    return jnp.where(in_window, bias, NEG_INF)


def _outproj_kernel(ya_ref, yb_ref, yc_ref, pb_ref, pc_ref, ph_ref, pcp_ref, php_ref,
                    cw_ref, w_ref, x_ref, o_ref, *, tiles_per_seq):
    tm = x_ref.shape[0]
    i = pl.program_id(0)
    u = pc_ref[...].astype(F32) * ph_ref[...].astype(F32)
    up = pcp_ref[...].astype(F32) * php_ref[...].astype(F32)
    up = jnp.where(i % tiles_per_seq == 0, 0.0, up)
    row = lax.broadcasted_iota(jnp.int32, (tm, 1), 0)
    s1 = jnp.where(row == 0, up[7:8, :], pltpu.roll(u, 1, axis=0))
    s2 = pltpu.roll(u, 2, axis=0)
    s2 = jnp.where(row == 0, up[6:7, :], jnp.where(row == 1, up[7:8, :], s2))
    cw = cw_ref[...]
    yd = pb_ref[...].astype(F32) * (cw[0:1, :] * s2 + cw[1:2, :] * s1 + cw[2:3, :] * u)
    acc = jnp.dot(ya_ref[...], w_ref[0:WIDTH, :], preferred_element_type=F32)
    acc += jnp.dot(yb_ref[...], w_ref[WIDTH:2 * WIDTH, :], preferred_element_type=F32)
    acc += jnp.dot(yc_ref[...], w_ref[2 * WIDTH:3 * WIDTH, :], preferred_element_type=F32)
    acc += jnp.dot(yd.astype(BF16), w_ref[3 * WIDTH:, :], preferred_element_type=F32)
    o_ref[...] = x_ref[...] + acc


def _outproj(ya, yb, yc, p, conv_w, w_bf16, x2d, seq, tm):
    n, d = x2d.shape
    db = D_OFF // WIDTH
    r8 = tm // 8
    ycur = pl.BlockSpec((tm, WIDTH), lambda i: (i, 0))
    pcur = lambda cb: pl.BlockSpec((tm, WIDTH), lambda i: (i, cb))
    pprev = lambda cb: pl.BlockSpec((8, WIDTH), lambda i: (jnp.maximum(i * r8 - 1, 0), cb))
    return pl.pallas_call(
        functools.partial(_outproj_kernel, tiles_per_seq=seq // tm),
        grid=(n // tm,),
        in_specs=[ycur, ycur, ycur, pcur(db), pcur(db + 1), pcur(db + 2),
                  pprev(db + 1), pprev(db + 2),
                  pl.BlockSpec(conv_w.shape, lambda i: (0, 0)),
                  pl.BlockSpec(w_bf16.shape, lambda i: (0, 0)),
                  pl.BlockSpec((tm, d), lambda i: (i, 0))],
        out_specs=pl.BlockSpec((tm, d), lambda i: (i, 0)),
        out_shape=jax.ShapeDtypeStruct((n, d), F32),
        compiler_params=_params(("parallel",)),
        name="outproj",
    )(ya, yb, yc, p, p, p, p, p, conv_w.astype(F32), w_bf16, x2d)


def _router_kernel(x_ref, g_ref, wh_ref, wl_ref, b_ref, route_ref):
    x = x_ref[...]
    ms = jnp.mean(x * x, axis=-1, keepdims=True)
    xn = x * lax.rsqrt(ms + RMS_EPS) * g_ref[...]
    xh = xn.astype(BF16)
    xl = (xn - xh.astype(F32)).astype(BF16)
    logits = (jnp.dot(xh, wh_ref[...], preferred_element_type=F32)
              + jnp.dot(xh, wl_ref[...], preferred_element_type=F32)
              + jnp.dot(xl, wh_ref[...], preferred_element_type=F32)) + b_ref[...]
    tm = x.shape[0]
    lane_i = lax.broadcasted_iota(jnp.int32, (tm, LANES), 1)
    lane = lane_i.astype(F32)
    lane_grp = lax.shift_right_logical(lane_i, 3).astype(F32)
    is_g = (lane_i >= N_EXPERTS) & (lane_i < N_EXPERTS + N_GROUPS)
    gl = jnp.where(is_g, logits, NEG_INF)
    gmax = jnp.max(gl, axis=-1, keepdims=True)
    gsum = jnp.sum(jnp.where(is_g, jnp.exp(gl - gmax), 0.0), axis=-1, keepdims=True)
    g_w = 1.0 / gsum
    g_idx = jnp.min(jnp.where(is_g & (gl == gmax), lane, float(LANES)), axis=-1,
                    keepdims=True) - float(N_EXPERTS)
    in_grp = (lane_i < N_EXPERTS) & (lane_grp == g_idx)
    el = jnp.where(in_grp, logits, NEG_INF)
    emax = jnp.max(el, axis=-1, keepdims=True)
    ee = jnp.where(in_grp, jnp.exp(el - emax), 0.0)
    ep = ee / jnp.sum(ee, axis=-1, keepdims=True)
    p1 = jnp.max(ep, axis=-1, keepdims=True)
    i1 = jnp.min(jnp.where(in_grp & (ep == p1), lane, float(LANES)), axis=-1, keepdims=True)
    rest = in_grp & (lane != i1)
    ep2 = jnp.where(rest, ep, -1.0)
    p2 = jnp.max(ep2, axis=-1, keepdims=True)
    i2 = jnp.min(jnp.where(rest & (ep2 == p2), lane, float(LANES)), axis=-1, keepdims=True)
    tot = p1 + p2
    route_ref[...] = (jnp.where(lane_i == ROUTE_ID1, i1, 0.0)
                      + jnp.where(lane_i == ROUTE_ID2, i2, 0.0)
                      + jnp.where(lane_i == ROUTE_W1, g_w * p1 / tot, 0.0)
                      + jnp.where(lane_i == ROUTE_W2, g_w * p2 / tot, 0.0))


def _router(x2d, gain, wr_hi, wr_lo, br, tm):
    n, d = x2d.shape
    return pl.pallas_call(
        _router_kernel,
        grid=(n // tm,),
        in_specs=[pl.BlockSpec((tm, d), lambda i: (i, 0)),
                  pl.BlockSpec((1, d), lambda i: (0, 0)),
                  pl.BlockSpec((d, LANES), lambda i: (0, 0)),
                  pl.BlockSpec((d, LANES), lambda i: (0, 0)),
                  pl.BlockSpec((1, LANES), lambda i: (0, 0))],
        out_specs=pl.BlockSpec((tm, LANES), lambda i: (i, 0)),
        out_shape=jax.ShapeDtypeStruct((n, LANES), F32),
        compiler_params=_params(("parallel",)),
        name="router",
    )(x2d, gain, wr_hi, wr_lo, br)


def _gather_pipeline(step, n_steps, n_live, idx_hbm, src_hbm, idx_smem, buf, isem, gsem):
    rows = buf.shape[1]

    def idx_copy(j, slot):
        return pltpu.make_async_copy(idx_hbm.at[pl.ds(j * IDX_TILE, IDX_TILE)],
                                     idx_smem.at[pl.ds(slot * IDX_TILE, IDX_TILE)],
                                     isem.at[slot])

    def row_copy(token, r, slot):
        return pltpu.make_async_copy(src_hbm.at[pl.ds(token, 1)],
                                     buf.at[slot, pl.ds(r, 1)], gsem.at[slot])

    def start_rows(slot):
        def body(r, carry):
            row_copy(idx_smem[slot * IDX_TILE + r], r, slot).start()
            return carry
        lax.fori_loop(0, rows, body, 0, unroll=8)

    def wait_rows(slot):
        def body(r, carry):
            row_copy(0, r, slot).wait()
            return carry
        lax.fori_loop(0, rows, body, 0, unroll=8)

    cur = lax.rem(step, 2)
    nxt = 1 - cur

    @pl.when(step == 0)
    def _():
        first = idx_copy(0, 0)
        first.start()
        first.wait()
        start_rows(0)
        if n_steps > 1:
            idx_copy(1, 1).start()

    @pl.when(step + 1 < n_steps)
    def _():
        idx_copy(step + 1, nxt).wait()

        @pl.when(step + 1 < n_live)
        def _():
            start_rows(nxt)

    @pl.when(step + 2 < n_steps)
    def _():
        idx_copy(step + 2, cur).start()

    @pl.when(step < n_live)
    def _():
        wait_rows(cur)

    return cur


def _experts_kernel(te_ref, nv_ref, idx_hbm, x_hbm, g_ref, wg_ref, wu_ref, wd_ref, y_ref,
                    idx_smem, xbuf, isem, gsem, *, n_steps):
    j = pl.program_id(0)
    n_live = nv_ref[0]
    cur = _gather_pipeline(j, n_steps, n_live, idx_hbm, x_hbm, idx_smem, xbuf, isem, gsem)

    @pl.when(j < n_live)
    def _():
        x = xbuf[cur]
        ms = jnp.mean(x * x, axis=-1, keepdims=True)
        xn = (x * lax.rsqrt(ms + RMS_EPS) * g_ref[...]).astype(BF16)
        hg = jnp.dot(xn, wg_ref[...], preferred_element_type=F32)
        hu = jnp.dot(xn, wu_ref[...], preferred_element_type=F32)
        h = hg * _sigmoid(hg) * hu
        y_ref[...] = jnp.dot(h.astype(BF16), wd_ref[...], preferred_element_type=F32)

    @pl.when(j >= n_live)
    def _():
        y_ref[...] = jnp.zeros_like(y_ref)


def _experts(x2d, gain, src_idx, tile_expert, n_live, wg, wu, wd, tm):
    n, d = x2d.shape
    _, _, hid = wg.shape
    n_steps = tile_expert.shape[0]
    return pl.pallas_call(
        functools.partial(_experts_kernel, n_steps=n_steps),
        grid_spec=pltpu.PrefetchScalarGridSpec(
            num_scalar_prefetch=2,
            grid=(n_steps,),
            in_specs=[pl.BlockSpec(memory_space=pl.ANY),
                      pl.BlockSpec(memory_space=pl.ANY),
                      pl.BlockSpec((1, d), lambda j, te, nv: (0, 0)),
                      pl.BlockSpec((None, d, hid), lambda j, te, nv: (te[j], 0, 0)),
                      pl.BlockSpec((None, d, hid), lambda j, te, nv: (te[j], 0, 0)),
                      pl.BlockSpec((None, hid, d), lambda j, te, nv: (te[j], 0, 0))],
            out_specs=pl.BlockSpec((tm, d), lambda j, te, nv: (j, 0)),
            scratch_shapes=[pltpu.SMEM((2 * IDX_TILE,), jnp.int32),
                            pltpu.VMEM((2, tm, d), F32),
                            pltpu.SemaphoreType.DMA((2,)),
                            pltpu.SemaphoreType.DMA((2,))]),
        out_shape=jax.ShapeDtypeStruct((n_steps * tm, d), F32),
        compiler_params=_params(("arbitrary",)),
        name="experts",
    )(tile_expert, n_live, src_idx, x2d, gain, wg, wu, wd)


def _combine_kernel(idx_hbm, y_hbm, x_ref, route_ref, gf_ref, o_ref,
                    idx_smem, ybuf, isem, gsem, *, n_steps, final_norm):
    j = pl.program_id(0)
    tm = x_ref.shape[0]
    cur = _gather_pipeline(j, n_steps, n_steps, idx_hbm, y_hbm, idx_smem, ybuf, isem, gsem)
    route = route_ref[...]
    w1 = route[:, ROUTE_W1:ROUTE_W1 + 1]
    w2 = route[:, ROUTE_W2:ROUTE_W2 + 1]
    y = x_ref[...] + w1 * ybuf[cur, 0:tm, :] + w2 * ybuf[cur, tm:2 * tm, :]
    if final_norm:
        ms = jnp.mean(y * y, axis=-1, keepdims=True)
        y = y * lax.rsqrt(ms + RMS_EPS) * gf_ref[...]
    o_ref[...] = y


def _combine(x2d, y_rows, pos_idx, route, gain_final, final_norm, tm):
    n, d = x2d.shape
    n_steps = n // tm
    return pl.pallas_call(
        functools.partial(_combine_kernel, n_steps=n_steps, final_norm=final_norm),
        grid=(n_steps,),
        in_specs=[pl.BlockSpec(memory_space=pl.ANY),
                  pl.BlockSpec(memory_space=pl.ANY),
                  pl.BlockSpec((tm, d), lambda j: (j, 0)),
                  pl.BlockSpec((tm, LANES), lambda j: (j, 0)),
                  pl.BlockSpec((1, d), lambda j: (0, 0))],
        out_specs=pl.BlockSpec((tm, d), lambda j: (j, 0)),
        out_shape=jax.ShapeDtypeStruct((n, d), F32),
        scratch_shapes=[pltpu.SMEM((2 * IDX_TILE,), jnp.int32),
                        pltpu.VMEM((2, 2 * tm, d), F32),
                        pltpu.SemaphoreType.DMA((2,)),
                        pltpu.SemaphoreType.DMA((2,))],
        compiler_params=_params(("arbitrary",)),
        name="combine",
    )(pos_idx, y_rows, x2d, route, gain_final)


def _dispatch_plan(e1, e2, tm_e, tm_c):
    n = e1.shape[0]
    n_tiles = (2 * n) // tm_e + N_EXPERTS
    e = jnp.concatenate([e1, e2])
    onehot = (e[:, None] == jnp.arange(N_EXPERTS, dtype=jnp.int32)[None, :]).astype(jnp.int32)
    csum = jnp.cumsum(onehot, axis=0)
    rank = jnp.sum((csum - onehot) * onehot, axis=1)
    counts = csum[-1]
    padded = ((counts + tm_e - 1) // tm_e) * tm_e
    ends = jnp.cumsum(padded)
    dest = jnp.sum((ends - padded)[None, :] * onehot, axis=1) + rank
    tok = jnp.arange(n, dtype=jnp.int32)
    src = jnp.zeros((n_tiles * tm_e,), jnp.int32).at[dest].set(jnp.concatenate([tok, tok]))
    src = jnp.pad(src.reshape(n_tiles, tm_e), ((0, 0), (0, IDX_TILE - tm_e))).reshape(-1)
    pos = jnp.concatenate([dest[:n].reshape(n // tm_c, tm_c), dest[n:].reshape(n // tm_c, tm_c)],
                          axis=1)
    pos = jnp.pad(pos, ((0, 0), (0, IDX_TILE - 2 * tm_c))).reshape(-1)
    tile_start = jnp.arange(n_tiles, dtype=jnp.int32) * tm_e
    tile_expert = jnp.minimum(jnp.sum((tile_start[:, None] >= ends[None, :]).astype(jnp.int32),
                                      axis=1), N_EXPERTS - 1)
    n_live = (ends[-1] // tm_e).reshape(1)
    return src, pos, tile_expert.astype(jnp.int32), n_live.astype(jnp.int32)


def _pick(total, pref):
    t = min(pref, total)
    while total % t:
        t //= 2
    return t


def kernel(x, w_in, w_out, norm_mix, norm_ffn, norm_final, rwkv_mu, rwkv_w0, rwkv_w2, rwkv_a0, rwkv_a2, rwkv_g2, rwkv_k_k, rwkv_k_a, rwkv_r_k, rwkv_gn_w, rwkv_gn_b, band_rel_bias, t5_rel_bias, diff_lambda_q1, diff_lambda_k1, diff_lambda_q2, diff_lambda_k2, diff_subln_w, conv_w, router_group_w, router_group_b, router_expert_w, router_expert_b, expert_w_gate, expert_w_up, expert_w_down):
    nb, seq, d = x.shape
    n = nb * seq
    depth = w_in.shape[0]
    x2d = x.reshape(n, d)

    tm_proj = _pick(n, 1024)
    tm_moe = _pick(n, 512)
    tm_exp = _pick(n, 256)
    t_rwkv = _pick(seq, 256)
    t_band = _pick(seq, 512)
    t_diff = _pick(seq, 512)
    row2 = lambda t: t.reshape(1, -1).astype(F32)

    t5_diag, t5_prev = _t5_tiles(t5_rel_bias, t_diff)

    for l in range(depth):
        wl = w_in[l]
        w_perm = jnp.concatenate([wl[:, :3 * WIDTH], wl[:, A_COLS:], wl[:, 3 * WIDTH:A_COLS]],
                                 axis=1).astype(BF16)
        p = _inproj(x2d, row2(norm_mix[l]), w_perm, tm_proj, 1280)

        ya = _rwkv(p, nb, seq, t_rwkv, rwkv_mu[l], rwkv_w0[l], rwkv_w2[l], rwkv_a0[l],
                   rwkv_a2[l], rwkv_g2[l], rwkv_k_k[l], rwkv_k_a[l], rwkv_r_k[l],
                   rwkv_gn_w[l], rwkv_gn_b[l])
        yb = _band(p, nb, seq, t_band,
                   _band_bias(band_rel_bias[l], B_BAND + CHUNK, 2 * CHUNK))
        lam_init = 0.8 - 0.6 * math.exp(-0.3 * l)
        yc = _diff(p, nb, seq, t_diff, t5_diag, t5_prev,
                   diff_lambda_q1[l], diff_lambda_k1[l], diff_lambda_q2[l],
                   diff_lambda_k2[l], diff_subln_w[l], lam_init)
        x2d = _outproj(ya, yb, yc, p, conv_w[l], w_out[l].astype(BF16), x2d, seq, tm_moe)

        wr = jnp.concatenate([router_expert_w[l], router_group_w[l]], axis=1).astype(F32)
        wr = jnp.pad(wr, ((0, 0), (0, LANES - wr.shape[1])))
        wr_hi = wr.astype(BF16)
        wr_lo = (wr - wr_hi.astype(F32)).astype(BF16)
        br = jnp.concatenate([router_expert_b[l], router_group_b[l]]).astype(F32)
        br = jnp.pad(br, (0, LANES - br.shape[0])).reshape(1, LANES)
        route = _router(x2d, row2(norm_ffn[l]), wr_hi, wr_lo, br, tm_moe)
        src_idx, pos_idx, tile_expert, n_live = _dispatch_plan(
            route[:, ROUTE_ID1].astype(jnp.int32), route[:, ROUTE_ID2].astype(jnp.int32),
            tm_exp, tm_moe)
        y_rows = _experts(x2d, row2(norm_ffn[l]), src_idx, tile_expert, n_live,
                          expert_w_gate[l].astype(BF16), expert_w_up[l].astype(BF16),
                          expert_w_down[l].astype(BF16), tm_exp)
        x2d = _combine(x2d, y_rows, pos_idx, route, row2(norm_final), l == depth - 1, tm_moe)
    return x2d.reshape(nb, seq, d)
```

```python
import functools
import math

import jax
import jax.numpy as jnp
from jax import lax
from jax.experimental import pallas as pl
from jax.experimental.pallas import tpu as pltpu

F32 = jnp.float32
BF16 = jnp.bfloat16

DEPTH = 2
CHUNK = 64
CHUNK_SHIFT = 6
HEAD_DIM = 64
A_HEADS = 8
B_HEADS = 8
C_HEADS = 4
WIDTH = 512
A_DECAY_LORA = 64
A_ICLR_LORA = 64
A_GATE_LORA = 128
A_LORA = A_DECAY_LORA + A_ICLR_LORA + A_GATE_LORA
A_COLS = 3 * WIDTH + A_LORA
B_OFF = 3 * WIDTH
C_OFF = B_OFF + 3 * WIDTH
D_OFF = C_OFF + 3 * WIDTH
LORA_OFF = D_OFF + 3 * WIDTH
IN_COLS = LORA_OFF + A_LORA
B_LEFT_CHUNKS = 8
B_BAND = (B_LEFT_CHUNKS + 1) * CHUNK
REL_CLIP = 128
T5_BUCKETS = 32
T5_MAX_DIST = 128
N_GROUPS = 4
EXPERTS_PER_GROUP = 8
N_EXPERTS = 32
RMS_EPS = 1e-6
RWKV_GN_EPS = 64e-5
SUBLN_EPS = 1e-5
NEG_INF = -1e30

LANES = 128
F32_SUBLANES = 8
BF16_SUBLANES = 16
IDX_TILE = 1024

ROUTE_ID1, ROUTE_ID2, ROUTE_W1, ROUTE_W2 = 0, 1, 2, 3
VMEM_LIMIT_BYTES = 56 * 1024 * 1024

_NT = (((1,), (1,)), ((), ()))
_TN = (((0,), (0,)), ((), ()))


def _params(semantics):
    return pltpu.CompilerParams(dimension_semantics=semantics,
                                vmem_limit_bytes=VMEM_LIMIT_BYTES)


def _bdot(a, b):
    return jnp.dot(a.astype(BF16), b.astype(BF16), preferred_element_type=F32)


def _sigmoid(x):
    return 1.0 / (1.0 + jnp.exp(-x))


def _inproj_kernel(x_ref, g_ref, w_ref, o_ref, xn_ref):
    @pl.when(pl.program_id(1) == 0)
    def _():
        x = x_ref[...]
        ms = jnp.mean(x * x, axis=-1, keepdims=True)
        xn_ref[...] = (x * lax.rsqrt(ms + RMS_EPS) * g_ref[...]).astype(BF16)

    o_ref[...] = jnp.dot(xn_ref[...], w_ref[...],
                         preferred_element_type=F32).astype(o_ref.dtype)


def _inproj(x2d, gain, w_bf16, tm, tn):
    n, d = x2d.shape
    cols = w_bf16.shape[1]
    return pl.pallas_call(
        _inproj_kernel,
        grid=(n // tm, cols // tn),
        in_specs=[pl.BlockSpec((tm, d), lambda i, j: (i, 0)),
                  pl.BlockSpec((1, d), lambda i, j: (0, 0)),
                  pl.BlockSpec((d, tn), lambda i, j: (0, j))],
        out_specs=pl.BlockSpec((tm, tn), lambda i, j: (i, j)),
        out_shape=jax.ShapeDtypeStruct((n, cols), BF16),
        scratch_shapes=[pltpu.VMEM((tm, d), BF16)],
        compiler_params=_params(("parallel", "arbitrary")),
        name="inproj",
    )(x2d, gain, w_bf16)


def _rwkv_kernel(p_ref, pl_ref, mu_ref, mul_ref, w0_ref, w2_ref, a0_ref, a2_ref, g2_ref,
                 kk_ref, ka_ref, rk_ref, gnw_ref, gnb_ref,
                 o_ref,
                 state_ref, prev_ref, prevl_ref, r_s, k_s, v_s, na_s, b_s, ld_s, cum_s,
                 y_s, g_s, bonus_s, x0_s, y0_s, mrb_s, tinv_s, ar_s, bke_s):
    tb = p_ref.shape[0]
    n_chunks = tb // CHUNK

    @pl.when(pl.program_id(1) == 0)
    def _():
        state_ref[...] = jnp.zeros_like(state_ref)
        prev_ref[...] = jnp.zeros_like(prev_ref)
        prevl_ref[...] = jnp.zeros_like(prevl_ref)

    row = lax.broadcasted_iota(jnp.int32, (tb, 1), 0)

    def token_shift(src_ref, last_ref, m_ref):
        pa = src_ref[...].astype(F32)
        shifted = jnp.where(row == 0, last_ref[...], pltpu.roll(pa, 1, axis=0))
        last_ref[...] = pa[tb - 1:tb, :]
        return pa + (shifted - pa) * m_ref[...]

    ps = token_shift(p_ref, prev_ref, mu_ref)
    lora = token_shift(pl_ref, prevl_ref, mul_ref)
    r = ps[:, 0:WIDTH]
    k = ps[:, WIDTH:2 * WIDTH]
    v = ps[:, 2 * WIDTH:3 * WIDTH]
    w_lo = lora[:, 0:A_DECAY_LORA]
    a_lo = lora[:, A_DECAY_LORA:A_DECAY_LORA + A_ICLR_LORA]
    g_lo = lora[:, A_DECAY_LORA + A_ICLR_LORA:A_LORA]

    z = -(w0_ref[...] + _bdot(jnp.tanh(w_lo), w2_ref[...]))
    softplus = jnp.maximum(z, 0.0) + jnp.log(1.0 + jnp.exp(-jnp.abs(z)))
    ld = -jnp.exp(-softplus - 0.5)
    a = _sigmoid(a0_ref[...] + _bdot(a_lo, a2_ref[...]))
    g_s[...] = _bdot(_sigmoid(g_lo), g2_ref[...])

    kk = k * kk_ref[...]
    kmod = k * (1.0 + (a - 1.0) * ka_ref[...])
    rkr = r * kmod * rk_ref[...]
    for h in range(A_HEADS):
        sl = slice(h * HEAD_DIM, (h + 1) * HEAD_DIM)
        kkh = kk[:, sl]
        nrm = jnp.sqrt(jnp.sum(kkh * kkh, axis=-1, keepdims=True))
        kkn = kkh / jnp.maximum(nrm, 1e-12)
        na_s[:, sl] = -kkn
        b_s[:, sl] = kkn * a[:, sl]
        bonus_s[:, sl] = jnp.sum(rkr[:, sl], axis=-1, keepdims=True) * v[:, sl]
    r_s[...] = r
    k_s[...] = kmod
    v_s[...] = v
    ld_s[...] = ld

    ri = lax.broadcasted_iota(jnp.int32, (tb, tb), 0)
    ci = lax.broadcasted_iota(jnp.int32, (tb, tb), 1)
    same_chunk = (lax.shift_right_logical(ri, CHUNK_SHIFT)
                  == lax.shift_right_logical(ci, CHUNK_SHIFT))
    tri = jnp.where((ri >= ci) & same_chunk, 1.0, 0.0).astype(BF16)
    ld_hi = ld.astype(BF16)
    ld_lo = (ld - ld_hi.astype(F32)).astype(BF16)
    cum_s[...] = (jnp.dot(tri, ld_hi, preferred_element_type=F32)
                  + jnp.dot(tri, ld_lo, preferred_element_type=F32))

    rc = lax.broadcasted_iota(jnp.int32, (CHUNK, CHUNK), 0)
    cc = lax.broadcasted_iota(jnp.int32, (CHUNK, CHUNK), 1)
    strict = rc > cc
    incl = rc >= cc
    eye = jnp.where(rc == cc, 1.0, 0.0)

    heads = range(A_HEADS)
    hsl = [slice(h * HEAD_DIM, (h + 1) * HEAD_DIM) for h in heads]

    def phase_a(c, carry):
        r0 = pl.multiple_of(c * CHUNK, CHUNK)
        rows = pl.ds(r0, CHUNK)
        cum = cum_s[rows, :]
        cum_last = cum[CHUNK - 1:CHUNK, :]
        w_inv = jnp.exp(-cum)
        w_end = jnp.exp(cum_last - cum)
        at = na_s[rows, :] * jnp.exp(cum - ld_s[rows, :])
        rt = r_s[rows, :] * jnp.exp(cum)
        bh = b_s[rows, :]
        kh = k_s[rows, :]
        vh = v_s[rows, :].astype(BF16)
        bt, kt = bh * w_inv, kh * w_inv
        be, ke = bh * w_end, kh * w_end
        ar = [jnp.concatenate([at[:, s], rt[:, s]], axis=0).astype(BF16) for s in hsl]
        bk = [jnp.concatenate([bt[:, s], kt[:, s]], axis=0).astype(BF16) for s in hsl]
        gram = [lax.dot_general(ar[h], bk[h], _NT, preferred_element_type=F32) for h in heads]
        l_ab = [jnp.where(strict, g[0:CHUNK, 0:CHUNK], 0.0) for g in gram]
        tinv = [eye + m for m in l_ab]
        mpow = l_ab
        for _ in range(5):
            mpow = [_bdot(m, m) for m in mpow]
            tinv = [t + _bdot(m, t) for m, t in zip(mpow, tinv)]
        for h in heads:
            idx = c * A_HEADS + h
            g = gram[h]
            vhh = vh[:, hsl[h]]
            l_ak = jnp.where(strict, g[0:CHUNK, CHUNK:], 0.0).astype(BF16)
            m_rk = jnp.where(incl, g[CHUNK:, CHUNK:], 0.0).astype(BF16)
            x0_s[idx] = jnp.dot(l_ak, vhh, preferred_element_type=F32)
            y0_s[idx] = jnp.dot(m_rk, vhh, preferred_element_type=F32)
            mrb_s[idx] = jnp.where(incl, g[CHUNK:, 0:CHUNK], 0.0).astype(BF16)
            tinv_s[idx] = tinv[h].astype(BF16)
            ar_s[idx] = ar[h]
            bke_s[idx] = jnp.concatenate([be[:, hsl[h]], ke[:, hsl[h]]], axis=0).astype(BF16)
        return carry

    lax.fori_loop(0, n_chunks, phase_a, 0)

    def phase_b(c, carry):
        r0 = pl.multiple_of(c * CHUNK, CHUNK)
        rows = pl.ds(r0, CHUNK)
        wc = jnp.exp(cum_s[rows, :][CHUNK - 1:CHUNK, :])
        vh = v_s[rows, :]
        idx = [c * A_HEADS + h for h in heads]
        s0 = [state_ref[h] for h in heads]
        ars = [lax.dot_general(ar_s[idx[h]], s0[h].astype(BF16), _NT,
                               preferred_element_type=F32) for h in heads]
        u = [jnp.dot(tinv_s[idx[h]], (ars[h][0:CHUNK] + x0_s[idx[h]]).astype(BF16),
                     preferred_element_type=F32) for h in heads]
        y = [ars[h][CHUNK:] + y0_s[idx[h]]
             + jnp.dot(mrb_s[idx[h]], u[h].astype(BF16), preferred_element_type=F32)
             for h in heads]
        for h in heads:
            uv = jnp.concatenate([u[h], vh[:, hsl[h]]], axis=0).astype(BF16)
            state_ref[h] = s0[h] * wc[:, hsl[h]] + lax.dot_general(
                uv, bke_s[idx[h]], _TN, preferred_element_type=F32)
        for h in heads:
            mean = jnp.mean(y[h], axis=-1, keepdims=True)
            yc = y[h] - mean
            var = jnp.mean(yc * yc, axis=-1, keepdims=True)
            y_s[rows, hsl[h]] = yc * lax.rsqrt(var + RWKV_GN_EPS)
        return carry

    lax.fori_loop(0, n_chunks, phase_b, 0)

    out = (y_s[...] * gnw_ref[...] + gnb_ref[...] + bonus_s[...]) * g_s[...]
    o_ref[...] = out.astype(o_ref.dtype)


def _rwkv(p, nb, seq, tb, mu, w0, w2, a0, a2, g2, k_k, k_a, r_k, gn_w, gn_b):
    n = p.shape[0]
    nt = seq // tb
    row2 = lambda t: t.reshape(1, -1).astype(F32)
    vec_spec = lambda width: pl.BlockSpec((1, width), lambda b, t: (0, 0))
    full = lambda arr: pl.BlockSpec(arr.shape, lambda b, t: (0, 0))
    w2b, a2b, g2b = w2.astype(BF16), a2.astype(BF16), g2.astype(BF16)
    scr = lambda: pltpu.VMEM((tb, WIDTH), F32)
    nch = (tb // CHUNK) * A_HEADS
    return pl.pallas_call(
        _rwkv_kernel,
        grid=(nb, nt),
        in_specs=[pl.BlockSpec((tb, 3 * WIDTH), lambda b, t: (b * nt + t, 0)),
                  pl.BlockSpec((tb, A_LORA), lambda b, t: (b * nt + t, LORA_OFF // A_LORA)),
                  vec_spec(3 * WIDTH), vec_spec(A_LORA), vec_spec(WIDTH), full(w2b),
                  vec_spec(WIDTH), full(a2b), full(g2b), vec_spec(WIDTH), vec_spec(WIDTH),
                  vec_spec(WIDTH), vec_spec(WIDTH), vec_spec(WIDTH)],
        out_specs=pl.BlockSpec((tb, WIDTH), lambda b, t: (b * nt + t, 0)),
        out_shape=jax.ShapeDtypeStruct((n, WIDTH), BF16),
        scratch_shapes=[pltpu.VMEM((A_HEADS, HEAD_DIM, HEAD_DIM), F32),
                        pltpu.VMEM((1, 3 * WIDTH), F32),
                        pltpu.VMEM((1, A_LORA), F32)] + [scr() for _ in range(10)]
        + [pltpu.VMEM((nch, CHUNK, CHUNK), F32), pltpu.VMEM((nch, CHUNK, CHUNK), F32),
           pltpu.VMEM((nch, CHUNK, CHUNK), BF16), pltpu.VMEM((nch, CHUNK, CHUNK), BF16),
           pltpu.VMEM((nch, 2 * CHUNK, HEAD_DIM), BF16),
           pltpu.VMEM((nch, 2 * CHUNK, HEAD_DIM), BF16)],
        compiler_params=_params(("parallel", "arbitrary")),
        name="rwkv",
    )(p, p, row2(mu[:3 * WIDTH]), row2(mu[3 * WIDTH:]), row2(w0), w2b, row2(a0), a2b, g2b,
      row2(k_k), row2(k_a), row2(r_k), row2(gn_w), row2(gn_b))


def _band_kernel(q_ref, kp_ref, kc_ref, vp_ref, vc_ref, bias_ref, o_ref, k_s, vt_s):
    tq = q_ref.shape[0]
    i = pl.program_id(1)
    pair = 2 * CHUNK
    nkeys = B_BAND + CHUNK
    hsl = [slice(h * HEAD_DIM, (h + 1) * HEAD_DIM) for h in range(B_HEADS)]

    pad_rows = vt_s.shape[1] - HEAD_DIM
    rid = lax.broadcasted_iota(jnp.int32, (pad_rows, 2 * tq), 0)
    ones_rows = jnp.where(rid == 0, 1.0, 0.0).astype(BF16)
    for half, (kr, vr) in enumerate(((kp_ref, vp_ref), (kc_ref, vc_ref))):
        rows = slice(half * tq, (half + 1) * tq)
        vt = vr[...].astype(F32).T
        for h in range(B_HEADS):
            k_s[h, rows, :] = kr[:, hsl[h]]
            vt_s[h, 0:HEAD_DIM, rows] = vt[hsl[h], :].astype(BF16)
    for h in range(B_HEADS):
        vt_s[h, HEAD_DIM:, :] = ones_rows

    q = q_ref[...] * (HEAD_DIM ** -0.5)
    krow = lax.broadcasted_iota(jnp.int32, (nkeys, pair), 0)
    for pi in range(tq // pair):
        off = pi * pair
        valid = jnp.logical_or(i > 0, krow + off >= tq)
        heads = range(B_HEADS)
        s = [lax.dot_general(k_s[h, off:off + nkeys, :], q[off:off + pair, hsl[h]], _NT,
                             preferred_element_type=F32) for h in heads]
        s = [jnp.where(valid, s[h] + bias_ref[h], NEG_INF) for h in heads]
        m = [jnp.max(s[h], axis=0, keepdims=True) for h in heads]
        e = [jnp.exp(s[h] - m[h]).astype(BF16) for h in heads]
        acc = [jnp.dot(vt_s[h, :, off:off + nkeys], e[h], preferred_element_type=F32)
               for h in heads]
        outs = [a[0:HEAD_DIM] / a[HEAD_DIM:HEAD_DIM + 1] for a in acc]
        o_ref[off:off + pair, :] = jnp.concatenate(outs, axis=0).T.astype(o_ref.dtype)


def _band(p, nb, seq, tq, bias):
    n = p.shape[0]
    nt = seq // tq
    qb, kb, vb = B_OFF // WIDTH, B_OFF // WIDTH + 1, B_OFF // WIDTH + 2
    cur = lambda cb: pl.BlockSpec((tq, WIDTH), lambda b, t: (b * nt + t, cb))
    prv = lambda cb: pl.BlockSpec((tq, WIDTH), lambda b, t: (b * nt + jnp.maximum(t - 1, 0), cb))
    return pl.pallas_call(
        _band_kernel,
        grid=(nb, nt),
        in_specs=[cur(qb), prv(kb), cur(kb), prv(vb), cur(vb),
                  pl.BlockSpec(bias.shape, lambda b, t: (0, 0, 0))],
        out_specs=pl.BlockSpec((tq, WIDTH), lambda b, t: (b * nt + t, 0)),
        out_shape=jax.ShapeDtypeStruct((n, WIDTH), BF16),
        scratch_shapes=[pltpu.VMEM((B_HEADS, 2 * tq, HEAD_DIM), BF16),
                        pltpu.VMEM((B_HEADS, HEAD_DIM + BF16_SUBLANES, 2 * tq), BF16)],
        compiler_params=_params(("parallel", "arbitrary")),
        name="band",
    )(p, p, p, p, p, bias)


def _diff_kernel(q_ref, k_ref, v_ref, bd_ref, bp_ref,
                 lq1_ref, lk1_ref, lq2_ref, lk2_ref, sub_ref, o_ref,
                 k_s, vt_s, m_s, acc_s, *, lam_init):
    tq = q_ref.shape[0]
    seq = k_ref.shape[0]
    hw = 2 * HEAD_DIM
    i = pl.program_id(2)

    @pl.when(i == 0)
    def _():
        for comp in range(2):
            k_s[comp] = k_ref[:, comp * HEAD_DIM:(comp + 1) * HEAD_DIM]
        pad_rows = vt_s.shape[1] - hw
        rid = lax.broadcasted_iota(jnp.int32, (pad_rows, tq), 0)
        ones_row = jnp.where(rid == 0, 1.0, 0.0).astype(BF16)
        for j in range(seq // tq):
            vt = v_ref[j * tq:(j + 1) * tq, :].astype(F32).T.astype(BF16)
            vt_s[j] = jnp.concatenate([vt, ones_row], axis=0)

    q = q_ref[...] * (HEAD_DIM ** -0.5)
    qs = (q[:, 0:HEAD_DIM], q[:, HEAD_DIM:])

    def scores(j, comp, bias, width=1):
        r0 = pl.multiple_of(j * tq, tq)
        kc = k_s[comp, pl.ds(r0, width * tq), :]
        s = lax.dot_general(kc, qs[comp], _NT, preferred_element_type=F32)
        return s if bias is None else s + bias

    comps = range(2)

    def first_tile(j, bias):
        s = [scores(j, c, bias) for c in comps]
        m = [jnp.max(s[c], axis=0, keepdims=True) for c in comps]
        e = [jnp.exp(s[c] - m[c]).astype(BF16) for c in comps]
        vt = vt_s[j]
        for c in comps:
            m_s[c] = m[c]
            acc_s[c] = jnp.dot(vt, e[c], preferred_element_type=F32)

    def next_tile(j, bias, width=1):
        s = [scores(j, c, bias, width) for c in comps]
        m_old = [m_s[c] for c in comps]
        m_new = [jnp.maximum(m_old[c], jnp.max(s[c], axis=0, keepdims=True)) for c in comps]
        e = [jnp.exp(s[c] - m_new[c]).astype(BF16) for c in comps]
        alpha = [jnp.exp(m_old[c] - m_new[c]) for c in comps]
        pv = [jnp.dot(vt_s[j], e[c][0:tq], preferred_element_type=F32) for c in comps]
        for t in range(1, width):
            pv = [pv[c] + jnp.dot(vt_s[j + t], e[c][t * tq:(t + 1) * tq],
                                  preferred_element_type=F32) for c in comps]
        for c in comps:
            m_s[c] = m_new[c]
            acc_s[c] = alpha[c] * acc_s[c] + pv[c]

    first_tile(i, bd_ref[...])

    @pl.when(i >= 1)
    def _():
        next_tile(i - 1, bp_ref[...])

    n_far = jnp.maximum(i - 1, 0)

    def far_pair(jj, carry):
        next_tile(2 * jj, None, width=2)
        return carry

    lax.fori_loop(0, lax.shift_right_logical(n_far, 1), far_pair, 0)

    @pl.when(lax.bitwise_and(n_far, 1) == 1)
    def _():
        next_tile(n_far - 1, None)

    lam = (jnp.exp(jnp.sum(lq1_ref[...] * lk1_ref[...], axis=-1, keepdims=True))
           - jnp.exp(jnp.sum(lq2_ref[...] * lk2_ref[...], axis=-1, keepdims=True))
           + lam_init)
    a1, a2 = acc_s[0], acc_s[1]
    out = a1[0:hw] / a1[hw:hw + 1] - lam * (a2[0:hw] / a2[hw:hw + 1])
    out = out * lax.rsqrt(jnp.mean(out * out, axis=0, keepdims=True) + SUBLN_EPS)
    out = out * sub_ref[...] * (1.0 - lam_init)
    o_ref[...] = out.T.astype(o_ref.dtype)


def _diff(p, nb, seq, tq, bias_diag, bias_prev, lq1, lk1, lq2, lk2, subw, lam_init):
    n = p.shape[0]
    nt = seq // tq
    hw = 2 * HEAD_DIM
    vrows = hw + BF16_SUBLANES
    qb, kb, vb = C_OFF // hw, (C_OFF + WIDTH) // hw, (C_OFF + 2 * WIDTH) // hw
    row2 = lambda t: t.reshape(1, -1).astype(F32)
    vec = lambda width: pl.BlockSpec((1, width), lambda b, h, t: (0, 0))
    return pl.pallas_call(
        functools.partial(_diff_kernel, lam_init=lam_init),
        grid=(nb, C_HEADS, nt),
        in_specs=[pl.BlockSpec((tq, hw), lambda b, h, t: (b * nt + t, qb + h)),
                  pl.BlockSpec((seq, hw), lambda b, h, t: (b, kb + h)),
                  pl.BlockSpec((seq, hw), lambda b, h, t: (b, vb + h)),
                  pl.BlockSpec((None, tq, tq), lambda b, h, t: (h, 0, 0)),
                  pl.BlockSpec((None, tq, tq), lambda b, h, t: (h, 0, 0)),
                  vec(HEAD_DIM), vec(HEAD_DIM), vec(HEAD_DIM), vec(HEAD_DIM),
                  pl.BlockSpec((hw, 1), lambda b, h, t: (0, 0))],
        out_specs=pl.BlockSpec((tq, hw), lambda b, h, t: (b * nt + t, h)),
        out_shape=jax.ShapeDtypeStruct((n, WIDTH), BF16),
        scratch_shapes=[pltpu.VMEM((2, seq, HEAD_DIM), BF16),
                        pltpu.VMEM((nt, vrows, tq), BF16),
                        pltpu.VMEM((2, 1, tq), F32),
                        pltpu.VMEM((2, vrows, tq), F32)],
        compiler_params=_params(("parallel", "parallel", "arbitrary")),
        name="diffattn",
    )(p, p, p, bias_diag, bias_prev, row2(lq1), row2(lk1), row2(lq2),
      row2(lk2), subw.reshape(-1, 1).astype(F32))


def _t5_buckets(rel):
    nb = T5_BUCKETS // 2
    max_exact = nb // 2
    ret = (rel > 0).astype(jnp.int32) * nb
    n = jnp.abs(rel)
    nf = jnp.maximum(n, 1).astype(jnp.float32)
    large = max_exact + (jnp.log(nf / max_exact) / math.log(T5_MAX_DIST / max_exact)
                         * (nb - max_exact)).astype(jnp.int32)
    large = jnp.minimum(large, nb - 1)
    return ret + jnp.where(n < max_exact, n, large)


def _toeplitz(vec, rows, cols):
    h, period = vec.shape
    flat = jnp.tile(vec, (1, rows))[:, :rows * (period - 1)]
    return flat.reshape(h, rows, period - 1)[:, :, :cols]


def _t5_tiles(t5_table, tq):
    tab = t5_table.astype(F32)
    period = 2 * tq
    m = jnp.arange(period)
    qk = jnp.where(m < tq, m, m - period)
    far = tab[_t5_buckets(jnp.int32(-2 * tq))]
    vec_d = tab[_t5_buckets(-qk)].T - far[:, None]
    vec_p = tab[_t5_buckets(-tq - qk)].T - far[:, None]
    diag = _toeplitz(vec_d, tq, tq)
    prev = _toeplitz(vec_p, tq, tq)
    qi = jnp.arange(tq)[None, :]
    ki = jnp.arange(tq)[:, None]
    diag = jnp.where((ki // CHUNK) <= (qi // CHUNK), diag, NEG_INF)
    return diag, prev


def _band_bias(rel_bias, keys, queries):
    period = keys + queries
    m = jnp.arange(period)
    qk = jnp.where(m < queries, m, m - period)
    dist = B_LEFT_CHUNKS * CHUNK + qk
    vec = rel_bias[:, jnp.clip(dist, -REL_CLIP, REL_CLIP) + REL_CLIP].astype(F32)
    bias = _toeplitz(vec, keys, queries)
    qi = jnp.arange(queries)[None, :]
    ki = jnp.arange(keys)[:, None]
    in_window = jnp.where(qi < CHUNK, ki < B_BAND, ki >= CHUNK)
    return jnp.where(in_window, bias, NEG_INF)


def _outproj_kernel(ya_ref, yb_ref, yc_ref, pb_ref, pc_ref, ph_ref, pcp_ref, php_ref,
                    cw_ref, w_ref, x_ref, o_ref, *, tiles_per_seq):
    tm = x_ref.shape[0]
    i = pl.program_id(0)
    u = pc_ref[...].astype(F32) * ph_ref[...].astype(F32)
    up = pcp_ref[...].astype(F32) * php_ref[...].astype(F32)
    up = jnp.where(i % tiles_per_seq == 0, 0.0, up)
    row = lax.broadcasted_iota(jnp.int32, (tm, 1), 0)
    s1 = jnp.where(row == 0, up[7:8, :], pltpu.roll(u, 1, axis=0))
    s2 = pltpu.roll(u, 2, axis=0)
    s2 = jnp.where(row == 0, up[6:7, :], jnp.where(row == 1, up[7:8, :], s2))
    cw = cw_ref[...]
    yd = pb_ref[...].astype(F32) * (cw[0:1, :] * s2 + cw[1:2, :] * s1 + cw[2:3, :] * u)
    acc = jnp.dot(ya_ref[...], w_ref[0:WIDTH, :], preferred_element_type=F32)
    acc += jnp.dot(yb_ref[...], w_ref[WIDTH:2 * WIDTH, :], preferred_element_type=F32)
    acc += jnp.dot(yc_ref[...], w_ref[2 * WIDTH:3 * WIDTH, :], preferred_element_type=F32)
    acc += jnp.dot(yd.astype(BF16), w_ref[3 * WIDTH:, :], preferred_element_type=F32)
    o_ref[...] = x_ref[...] + acc


def _outproj(ya, yb, yc, p, conv_w, w_bf16, x2d, seq, tm):
    n, d = x2d.shape
    db = D_OFF // WIDTH
    r8 = tm // 8
    ycur = pl.BlockSpec((tm, WIDTH), lambda i: (i, 0))
    pcur = lambda cb: pl.BlockSpec((tm, WIDTH), lambda i: (i, cb))
    pprev = lambda cb: pl.BlockSpec((8, WIDTH), lambda i: (jnp.maximum(i * r8 - 1, 0), cb))
    return pl.pallas_call(
        functools.partial(_outproj_kernel, tiles_per_seq=seq // tm),
        grid=(n // tm,),
        in_specs=[ycur, ycur, ycur, pcur(db), pcur(db + 1), pcur(db + 2),
                  pprev(db + 1), pprev(db + 2),
                  pl.BlockSpec(conv_w.shape, lambda i: (0, 0)),
                  pl.BlockSpec(w_bf16.shape, lambda i: (0, 0)),
                  pl.BlockSpec((tm, d), lambda i: (i, 0))],
        out_specs=pl.BlockSpec((tm, d), lambda i: (i, 0)),
        out_shape=jax.ShapeDtypeStruct((n, d), F32),
        compiler_params=_params(("parallel",)),
        name="outproj",
    )(ya, yb, yc, p, p, p, p, p, conv_w.astype(F32), w_bf16, x2d)


def _router_kernel(x_ref, g_ref, wh_ref, wl_ref, b_ref, route_ref):
    x = x_ref[...]
    ms = jnp.mean(x * x, axis=-1, keepdims=True)
    xn = x * lax.rsqrt(ms + RMS_EPS) * g_ref[...]
    xh = xn.astype(BF16)
    xl = (xn - xh.astype(F32)).astype(BF16)
    logits = (jnp.dot(xh, wh_ref[...], preferred_element_type=F32)
              + jnp.dot(xh, wl_ref[...], preferred_element_type=F32)
              + jnp.dot(xl, wh_ref[...], preferred_element_type=F32)) + b_ref[...]
    tm = x.shape[0]
    lane_i = lax.broadcasted_iota(jnp.int32, (tm, LANES), 1)
    lane = lane_i.astype(F32)
    lane_grp = lax.shift_right_logical(lane_i, 3).astype(F32)
    is_g = (lane_i >= N_EXPERTS) & (lane_i < N_EXPERTS + N_GROUPS)
    gl = jnp.where(is_g, logits, NEG_INF)
    gmax = jnp.max(gl, axis=-1, keepdims=True)
    gsum = jnp.sum(jnp.where(is_g, jnp.exp(gl - gmax), 0.0), axis=-1, keepdims=True)
    g_w = 1.0 / gsum
    g_idx = jnp.min(jnp.where(is_g & (gl == gmax), lane, float(LANES)), axis=-1,
                    keepdims=True) - float(N_EXPERTS)
    in_grp = (lane_i < N_EXPERTS) & (lane_grp == g_idx)
    el = jnp.where(in_grp, logits, NEG_INF)
    emax = jnp.max(el, axis=-1, keepdims=True)
    ee = jnp.where(in_grp, jnp.exp(el - emax), 0.0)
    ep = ee / jnp.sum(ee, axis=-1, keepdims=True)
    p1 = jnp.max(ep, axis=-1, keepdims=True)
    i1 = jnp.min(jnp.where(in_grp & (ep == p1), lane, float(LANES)), axis=-1, keepdims=True)
    rest = in_grp & (lane != i1)
    ep2 = jnp.where(rest, ep, -1.0)
    p2 = jnp.max(ep2, axis=-1, keepdims=True)
    i2 = jnp.min(jnp.where(rest & (ep2 == p2), lane, float(LANES)), axis=-1, keepdims=True)
    tot = p1 + p2
    route_ref[...] = (jnp.where(lane_i == ROUTE_ID1, i1, 0.0)
                      + jnp.where(lane_i == ROUTE_ID2, i2, 0.0)
                      + jnp.where(lane_i == ROUTE_W1, g_w * p1 / tot, 0.0)
                      + jnp.where(lane_i == ROUTE_W2, g_w * p2 / tot, 0.0))


def _router(x2d, gain, wr_hi, wr_lo, br, tm):
    n, d = x2d.shape
    return pl.pallas_call(
        _router_kernel,
        grid=(n // tm,),
        in_specs=[pl.BlockSpec((tm, d), lambda i: (i, 0)),
                  pl.BlockSpec((1, d), lambda i: (0, 0)),
                  pl.BlockSpec((d, LANES), lambda i: (0, 0)),
                  pl.BlockSpec((d, LANES), lambda i: (0, 0)),
                  pl.BlockSpec((1, LANES), lambda i: (0, 0))],
        out_specs=pl.BlockSpec((tm, LANES), lambda i: (i, 0)),
        out_shape=jax.ShapeDtypeStruct((n, LANES), F32),
        compiler_params=_params(("parallel",)),
        name="router",
    )(x2d, gain, wr_hi, wr_lo, br)


def _gather_pipeline(step, n_steps, n_live, idx_hbm, src_hbm, idx_smem, buf, isem, gsem):
    groups = buf.shape[1]

    def idx_copy(j, slot):
        return pltpu.make_async_copy(idx_hbm.at[pl.ds(j * IDX_TILE, IDX_TILE)],
                                     idx_smem.at[pl.ds(slot * IDX_TILE, IDX_TILE)],
                                     isem.at[slot])

    def row_copy(token, g, sub, slot):
        src = src_hbm.at[lax.shift_right_logical(token, 3),
                         pl.ds(lax.bitwise_and(token, F32_SUBLANES - 1), 1)]
        return pltpu.make_async_copy(src, buf.at[slot, g, pl.ds(sub, 1)], gsem.at[slot])

    def start_rows(slot):
        def body(g, carry):
            base = slot * IDX_TILE + g * F32_SUBLANES
            for sub in range(F32_SUBLANES):
                row_copy(idx_smem[base + sub], g, sub, slot).start(priority=sub % 2)
            return carry
        lax.fori_loop(0, groups, body, 0)

    def wait_rows(slot):
        def body(g, carry):
            for sub in range(F32_SUBLANES):
                row_copy(0, g, sub, slot).wait()
            return carry
        lax.fori_loop(0, groups, body, 0)

    cur = lax.rem(step, 2)
    nxt = 1 - cur

    @pl.when(step == 0)
    def _():
        first = idx_copy(0, 0)
        first.start()
        first.wait()
        start_rows(0)
        if n_steps > 1:
            idx_copy(1, 1).start()

    @pl.when(step + 1 < n_steps)
    def _():
        idx_copy(step + 1, nxt).wait()

        @pl.when(step + 1 < n_live)
        def _():
            start_rows(nxt)

    @pl.when(step + 2 < n_steps)
    def _():
        idx_copy(step + 2, cur).start()

    @pl.when(step < n_live)
    def _():
        wait_rows(cur)

    return cur


def _experts_kernel(te_ref, nv_ref, idx_hbm, x_hbm, g_ref, wg_ref, wu_ref, wd_ref, y_ref,
                    idx_smem, xbuf, isem, gsem, wg_s, wu_s, wd_s, *, n_steps):
    j = pl.program_id(0)
    n_live = nv_ref[0]
    cur = _gather_pipeline(j, n_steps, n_live, idx_hbm, x_hbm, idx_smem, xbuf, isem, gsem)

    @pl.when(jnp.logical_or(j == 0, te_ref[j] != te_ref[jnp.maximum(j - 1, 0)]))
    def _():
        wg_s[...] = wg_ref[...].astype(BF16)
        wu_s[...] = wu_ref[...].astype(BF16)
        wd_s[...] = wd_ref[...].astype(BF16)

    @pl.when(j < n_live)
    def _():
        x = xbuf[cur].reshape(y_ref.shape)
        ms = jnp.mean(x * x, axis=-1, keepdims=True)
        xn = (x * lax.rsqrt(ms + RMS_EPS) * g_ref[...]).astype(BF16)
        hg = jnp.dot(xn, wg_s[...], preferred_element_type=F32)
        hu = jnp.dot(xn, wu_s[...], preferred_element_type=F32)
        h = hg * _sigmoid(hg) * hu
        y_ref[...] = jnp.dot(h.astype(BF16), wd_s[...], preferred_element_type=F32)

    @pl.when(j >= n_live)
    def _():
        y_ref[...] = jnp.zeros_like(y_ref)


def _experts(x2d, gain, src_idx, tile_expert, n_live, wg, wu, wd, layer, tm):
    n, d = x2d.shape
    hid = wg.shape[-1]
    n_steps = tile_expert.shape[0]
    return pl.pallas_call(
        functools.partial(_experts_kernel, n_steps=n_steps),
        grid_spec=pltpu.PrefetchScalarGridSpec(
            num_scalar_prefetch=2,
            grid=(n_steps,),
            in_specs=[pl.BlockSpec(memory_space=pl.ANY),
                      pl.BlockSpec(memory_space=pl.ANY),
                      pl.BlockSpec((1, d), lambda j, te, nv: (0, 0)),
                      pl.BlockSpec((None, None, d, hid), lambda j, te, nv: (layer, te[j], 0, 0)),
                      pl.BlockSpec((None, None, d, hid), lambda j, te, nv: (layer, te[j], 0, 0)),
                      pl.BlockSpec((None, None, hid, d), lambda j, te, nv: (layer, te[j], 0, 0))],
            out_specs=pl.BlockSpec((tm, d), lambda j, te, nv: (j, 0)),
            scratch_shapes=[pltpu.SMEM((2 * IDX_TILE,), jnp.int32),
                            pltpu.VMEM((2, tm // F32_SUBLANES, F32_SUBLANES, d), F32),
                            pltpu.SemaphoreType.DMA((2,)),
                            pltpu.SemaphoreType.DMA((2,)),
                            pltpu.VMEM((d, hid), BF16), pltpu.VMEM((d, hid), BF16),
                            pltpu.VMEM((hid, d), BF16)]),
        out_shape=jax.ShapeDtypeStruct((n_steps * tm, d), F32),
        compiler_params=_params(("arbitrary",)),
        name="experts",
    )(tile_expert, n_live, src_idx, x2d.reshape(n // F32_SUBLANES, F32_SUBLANES, d),
      gain, wg, wu, wd)


def _combine_kernel(idx_hbm, y_hbm, x_ref, route_ref, gf_ref, o_ref,
                    idx_smem, ybuf, isem, gsem, *, n_steps, final_norm):
    j = pl.program_id(0)
    tm = x_ref.shape[0]
    cur = _gather_pipeline(j, n_steps, n_steps, idx_hbm, y_hbm, idx_smem, ybuf, isem, gsem)
    route = route_ref[...]
    w1 = route[:, ROUTE_W1:ROUTE_W1 + 1]
    w2 = route[:, ROUTE_W2:ROUTE_W2 + 1]
    yb = ybuf[cur].reshape(2 * tm, x_ref.shape[1])
    y = x_ref[...] + w1 * yb[0:tm] + w2 * yb[tm:]
    if final_norm:
        ms = jnp.mean(y * y, axis=-1, keepdims=True)
        y = y * lax.rsqrt(ms + RMS_EPS) * gf_ref[...]
    o_ref[...] = y


def _combine(x2d, y_rows, pos_idx, route, gain_final, final_norm, tm):
    n, d = x2d.shape
    n_steps = n // tm
    return pl.pallas_call(
        functools.partial(_combine_kernel, n_steps=n_steps, final_norm=final_norm),
        grid=(n_steps,),
        in_specs=[pl.BlockSpec(memory_space=pl.ANY),
                  pl.BlockSpec(memory_space=pl.ANY),
                  pl.BlockSpec((tm, d), lambda j: (j, 0)),
                  pl.BlockSpec((tm, LANES), lambda j: (j, 0)),
                  pl.BlockSpec((1, d), lambda j: (0, 0))],
        out_specs=pl.BlockSpec((tm, d), lambda j: (j, 0)),
        out_shape=jax.ShapeDtypeStruct((n, d), F32),
        scratch_shapes=[pltpu.SMEM((2 * IDX_TILE,), jnp.int32),
                        pltpu.VMEM((2, 2 * tm // F32_SUBLANES, F32_SUBLANES, d), F32),
                        pltpu.SemaphoreType.DMA((2,)),
                        pltpu.SemaphoreType.DMA((2,))],
        compiler_params=_params(("arbitrary",)),
        name="combine",
    )(pos_idx, y_rows.reshape(-1, F32_SUBLANES, d), x2d, route, gain_final)


def _dispatch_plan(e1, e2, tm_e, tm_c):
    n = e1.shape[0]
    n_tiles = (2 * n) // tm_e + N_EXPERTS
    e = jnp.concatenate([e1, e2])
    onehot = (e[:, None] == jnp.arange(N_EXPERTS, dtype=jnp.int32)[None, :]).astype(jnp.int32)
    csum = jnp.cumsum(onehot, axis=0)
    rank = jnp.sum((csum - onehot) * onehot, axis=1)
    counts = csum[-1]
    padded = ((counts + tm_e - 1) // tm_e) * tm_e
    ends = jnp.cumsum(padded)
    dest = jnp.sum((ends - padded)[None, :] * onehot, axis=1) + rank
    tok = jnp.arange(n, dtype=jnp.int32)
    src = jnp.zeros((n_tiles * tm_e,), jnp.int32).at[dest].set(jnp.concatenate([tok, tok]))
    src = jnp.pad(src.reshape(n_tiles, tm_e), ((0, 0), (0, IDX_TILE - tm_e))).reshape(-1)
    pos = jnp.concatenate([dest[:n].reshape(n // tm_c, tm_c), dest[n:].reshape(n // tm_c, tm_c)],
                          axis=1)
    pos = jnp.pad(pos, ((0, 0), (0, IDX_TILE - 2 * tm_c))).reshape(-1)
    tile_start = jnp.arange(n_tiles, dtype=jnp.int32) * tm_e
    tile_expert = jnp.minimum(jnp.sum((tile_start[:, None] >= ends[None, :]).astype(jnp.int32),
                                      axis=1), N_EXPERTS - 1)
    n_live = (ends[-1] // tm_e).reshape(1)
    return src, pos, tile_expert.astype(jnp.int32), n_live.astype(jnp.int32)


def _pick(total, pref):
    t = min(pref, total)
    while total % t:
        t //= 2
    return t


def kernel(x, w_in, w_out, norm_mix, norm_ffn, norm_final, rwkv_mu, rwkv_w0, rwkv_w2, rwkv_a0, rwkv_a2, rwkv_g2, rwkv_k_k, rwkv_k_a, rwkv_r_k, rwkv_gn_w, rwkv_gn_b, band_rel_bias, t5_rel_bias, diff_lambda_q1, diff_lambda_k1, diff_lambda_q2, diff_lambda_k2, diff_subln_w, conv_w, router_group_w, router_group_b, router_expert_w, router_expert_b, expert_w_gate, expert_w_up, expert_w_down):
    nb, seq, d = x.shape
    n = nb * seq
    depth = w_in.shape[0]
    x2d = x.reshape(n, d)

    tm_proj = _pick(n, 1024)
    tm_moe = _pick(n, 512)
    tm_exp = _pick(n, 256)
    t_rwkv = _pick(seq, 256)
    t_band = _pick(seq, 512)
    t_diff = _pick(seq, 512)
    row2 = lambda t: t.reshape(1, -1).astype(F32)

    t5_diag, t5_prev = _t5_tiles(t5_rel_bias, t_diff)

    for l in range(depth):
        wl = w_in[l]
        w_perm = jnp.concatenate([wl[:, :3 * WIDTH], wl[:, A_COLS:], wl[:, 3 * WIDTH:A_COLS]],
                                 axis=1).astype(BF16)
        p = _inproj(x2d, row2(norm_mix[l]), w_perm, tm_proj, 1280)

        ya = _rwkv(p, nb, seq, t_rwkv, rwkv_mu[l], rwkv_w0[l], rwkv_w2[l], rwkv_a0[l],
                   rwkv_a2[l], rwkv_g2[l], rwkv_k_k[l], rwkv_k_a[l], rwkv_r_k[l],
                   rwkv_gn_w[l], rwkv_gn_b[l])
        yb = _band(p, nb, seq, t_band,
                   _band_bias(band_rel_bias[l], B_BAND + CHUNK, 2 * CHUNK))
        lam_init = 0.8 - 0.6 * math.exp(-0.3 * l)
        yc = _diff(p, nb, seq, t_diff, t5_diag, t5_prev,
                   diff_lambda_q1[l], diff_lambda_k1[l], diff_lambda_q2[l],
                   diff_lambda_k2[l], diff_subln_w[l], lam_init)
        x2d = _outproj(ya, yb, yc, p, conv_w[l], w_out[l].astype(BF16), x2d, seq, tm_moe)

        wr = jnp.concatenate([router_expert_w[l], router_group_w[l]], axis=1).astype(F32)
        wr = jnp.pad(wr, ((0, 0), (0, LANES - wr.shape[1])))
        wr_hi = wr.astype(BF16)
        wr_lo = (wr - wr_hi.astype(F32)).astype(BF16)
        br = jnp.concatenate([router_expert_b[l], router_group_b[l]]).astype(F32)
        br = jnp.pad(br, (0, LANES - br.shape[0])).reshape(1, LANES)
        route = _router(x2d, row2(norm_ffn[l]), wr_hi, wr_lo, br, tm_moe)
        src_idx, pos_idx, tile_expert, n_live = _dispatch_plan(
            route[:, ROUTE_ID1].astype(jnp.int32), route[:, ROUTE_ID2].astype(jnp.int32),
            tm_exp, tm_moe)
        y_rows = _experts(x2d, row2(norm_ffn[l]), src_idx, tile_expert, n_live,
                          expert_w_gate, expert_w_up, expert_w_down, l, tm_exp)
        x2d = _combine(x2d, y_rows, pos_idx, route, row2(norm_final), l == depth - 1, tm_moe)
    return x2d.reshape(nb, seq, d)
```

```python
import functools
import math

import jax
import jax.numpy as jnp
from jax import lax
from jax.experimental import pallas as pl
from jax.experimental.pallas import tpu as pltpu

F32 = jnp.float32
BF16 = jnp.bfloat16

DEPTH = 2
CHUNK = 64
CHUNK_SHIFT = 6
HEAD_DIM = 64
A_HEADS = 8
B_HEADS = 8
C_HEADS = 4
WIDTH = 512
A_DECAY_LORA = 64
A_ICLR_LORA = 64
A_GATE_LORA = 128
A_LORA = A_DECAY_LORA + A_ICLR_LORA + A_GATE_LORA
A_COLS = 3 * WIDTH + A_LORA
B_OFF = 3 * WIDTH
C_OFF = B_OFF + 3 * WIDTH
D_OFF = C_OFF + 3 * WIDTH
LORA_OFF = D_OFF + 3 * WIDTH
IN_COLS = LORA_OFF + A_LORA
B_LEFT_CHUNKS = 8
B_BAND = (B_LEFT_CHUNKS + 1) * CHUNK
REL_CLIP = 128
T5_BUCKETS = 32
T5_MAX_DIST = 128
N_GROUPS = 4
EXPERTS_PER_GROUP = 8
N_EXPERTS = 32
RMS_EPS = 1e-6
RWKV_GN_EPS = 64e-5
SUBLN_EPS = 1e-5
NEG_INF = -1e30

LANES = 128
F32_SUBLANES = 8
BF16_SUBLANES = 16
IDX_TILE = 1024

ROUTE_ID1, ROUTE_ID2, ROUTE_W1, ROUTE_W2 = 0, 1, 2, 3
VMEM_LIMIT_BYTES = 56 * 1024 * 1024

_NT = (((1,), (1,)), ((), ()))
_TN = (((0,), (0,)), ((), ()))


def _params(semantics):
    return pltpu.CompilerParams(dimension_semantics=semantics,
                                vmem_limit_bytes=VMEM_LIMIT_BYTES)


def _bdot(a, b):
    return jnp.dot(a.astype(BF16), b.astype(BF16), preferred_element_type=F32)


def _sigmoid(x):
    return 1.0 / (1.0 + jnp.exp(-x))


def _inproj_kernel(x_ref, g_ref, w_ref, o_ref, xn_ref):
    @pl.when(pl.program_id(1) == 0)
    def _():
        x = x_ref[...]
        ms = jnp.mean(x * x, axis=-1, keepdims=True)
        xn_ref[...] = (x * lax.rsqrt(ms + RMS_EPS) * g_ref[...]).astype(BF16)

    o_ref[...] = jnp.dot(xn_ref[...], w_ref[...],
                         preferred_element_type=F32).astype(o_ref.dtype)


def _inproj(x2d, gain, w_bf16, tm, tn):
    n, d = x2d.shape
    cols = w_bf16.shape[1]
    return pl.pallas_call(
        _inproj_kernel,
        grid=(n // tm, cols // tn),
        in_specs=[pl.BlockSpec((tm, d), lambda i, j: (i, 0)),
                  pl.BlockSpec((1, d), lambda i, j: (0, 0)),
                  pl.BlockSpec((d, tn), lambda i, j: (0, j))],
        out_specs=pl.BlockSpec((tm, tn), lambda i, j: (i, j)),
        out_shape=jax.ShapeDtypeStruct((n, cols), BF16),
        scratch_shapes=[pltpu.VMEM((tm, d), BF16)],
        compiler_params=_params(("parallel", "arbitrary")),
        name="inproj",
    )(x2d, gain, w_bf16)


def _rwkv_kernel(p_ref, pl_ref, mu_ref, mul_ref, w0_ref, w2_ref, a0_ref, a2_ref, g2_ref,
                 kk_ref, ka_ref, rk_ref, gnw_ref, gnb_ref,
                 o_ref,
                 state_ref, prev_ref, prevl_ref, r_s, k_s, v_s, na_s, b_s, ld_s, cum_s,
                 y_s, g_s, bonus_s, x0_s, y0_s, mrb_s, tinv_s, ar_s, bke_s):
    tb = p_ref.shape[0]
    n_chunks = tb // CHUNK

    @pl.when(pl.program_id(1) == 0)
    def _():
        state_ref[...] = jnp.zeros_like(state_ref)
        prev_ref[...] = jnp.zeros_like(prev_ref)
        prevl_ref[...] = jnp.zeros_like(prevl_ref)

    row = lax.broadcasted_iota(jnp.int32, (tb, 1), 0)

    def token_shift(src_ref, last_ref, m_ref):
        pa = src_ref[...].astype(F32)
        shifted = jnp.where(row == 0, last_ref[...], pltpu.roll(pa, 1, axis=0))
        last_ref[...] = pa[tb - 1:tb, :]
        return pa + (shifted - pa) * m_ref[...]

    ps = token_shift(p_ref, prev_ref, mu_ref)
    lora = token_shift(pl_ref, prevl_ref, mul_ref)
    r = ps[:, 0:WIDTH]
    k = ps[:, WIDTH:2 * WIDTH]
    v = ps[:, 2 * WIDTH:3 * WIDTH]
    w_lo = lora[:, 0:A_DECAY_LORA]
    a_lo = lora[:, A_DECAY_LORA:A_DECAY_LORA + A_ICLR_LORA]
    g_lo = lora[:, A_DECAY_LORA + A_ICLR_LORA:A_LORA]

    z = -(w0_ref[...] + _bdot(jnp.tanh(w_lo), w2_ref[...]))
    softplus = jnp.maximum(z, 0.0) + jnp.log(1.0 + jnp.exp(-jnp.abs(z)))
    ld = -jnp.exp(-softplus - 0.5)
    a = _sigmoid(a0_ref[...] + _bdot(a_lo, a2_ref[...]))
    g_s[...] = _bdot(_sigmoid(g_lo), g2_ref[...])

    kk = k * kk_ref[...]
    kmod = k * (1.0 + (a - 1.0) * ka_ref[...])
    rkr = r * kmod * rk_ref[...]
    for h in range(A_HEADS):
        sl = slice(h * HEAD_DIM, (h + 1) * HEAD_DIM)
        kkh = kk[:, sl]
        nrm = jnp.sqrt(jnp.sum(kkh * kkh, axis=-1, keepdims=True))
        kkn = kkh / jnp.maximum(nrm, 1e-12)
        na_s[:, sl] = -kkn
        b_s[:, sl] = kkn * a[:, sl]
        bonus_s[:, sl] = jnp.sum(rkr[:, sl], axis=-1, keepdims=True) * v[:, sl]
    r_s[...] = r
    k_s[...] = kmod
    v_s[...] = v
    ld_s[...] = ld

    ri = lax.broadcasted_iota(jnp.int32, (tb, tb), 0)
    ci = lax.broadcasted_iota(jnp.int32, (tb, tb), 1)
    same_chunk = (lax.shift_right_logical(ri, CHUNK_SHIFT)
                  == lax.shift_right_logical(ci, CHUNK_SHIFT))
    tri = jnp.where((ri >= ci) & same_chunk, 1.0, 0.0).astype(BF16)
    ld_hi = ld.astype(BF16)
    ld_lo = (ld - ld_hi.astype(F32)).astype(BF16)
    cum_s[...] = (jnp.dot(tri, ld_hi, preferred_element_type=F32)
                  + jnp.dot(tri, ld_lo, preferred_element_type=F32))

    rc = lax.broadcasted_iota(jnp.int32, (CHUNK, CHUNK), 0)
    cc = lax.broadcasted_iota(jnp.int32, (CHUNK, CHUNK), 1)
    strict = rc > cc
    incl = rc >= cc
    eye = jnp.where(rc == cc, 1.0, 0.0)

    heads = range(A_HEADS)
    hsl = [slice(h * HEAD_DIM, (h + 1) * HEAD_DIM) for h in heads]

    def phase_a(c, carry):
        r0 = pl.multiple_of(c * CHUNK, CHUNK)
        rows = pl.ds(r0, CHUNK)
        cum = cum_s[rows, :]
        cum_last = cum[CHUNK - 1:CHUNK, :]
        w_inv = jnp.exp(-cum)
        w_end = jnp.exp(cum_last - cum)
        at = na_s[rows, :] * jnp.exp(cum - ld_s[rows, :])
        rt = r_s[rows, :] * jnp.exp(cum)
        bh = b_s[rows, :]
        kh = k_s[rows, :]
        vh = v_s[rows, :].astype(BF16)
        bt, kt = bh * w_inv, kh * w_inv
        be, ke = bh * w_end, kh * w_end
        ar = [jnp.concatenate([at[:, s], rt[:, s]], axis=0).astype(BF16) for s in hsl]
        bk = [jnp.concatenate([bt[:, s], kt[:, s]], axis=0).astype(BF16) for s in hsl]
        gram = [lax.dot_general(ar[h], bk[h], _NT, preferred_element_type=F32) for h in heads]
        l_ab = [jnp.where(strict, g[0:CHUNK, 0:CHUNK], 0.0) for g in gram]
        tinv = [eye + m for m in l_ab]
        mpow = l_ab
        for _ in range(5):
            mpow = [_bdot(m, m) for m in mpow]
            tinv = [t + _bdot(m, t) for m, t in zip(mpow, tinv)]
        for h in heads:
            idx = c * A_HEADS + h
            g = gram[h]
            vhh = vh[:, hsl[h]]
            l_ak = jnp.where(strict, g[0:CHUNK, CHUNK:], 0.0).astype(BF16)
            m_rk = jnp.where(incl, g[CHUNK:, CHUNK:], 0.0).astype(BF16)
            x0_s[idx] = jnp.dot(l_ak, vhh, preferred_element_type=F32)
            y0_s[idx] = jnp.dot(m_rk, vhh, preferred_element_type=F32)
            mrb_s[idx] = jnp.where(incl, g[CHUNK:, 0:CHUNK], 0.0).astype(BF16)
            tinv_s[idx] = tinv[h].astype(BF16)
            ar_s[idx] = ar[h]
            bke_s[idx] = jnp.concatenate([be[:, hsl[h]], ke[:, hsl[h]]], axis=0).astype(BF16)
        return carry

    lax.fori_loop(0, n_chunks, phase_a, 0)

    def phase_b(c, carry):
        r0 = pl.multiple_of(c * CHUNK, CHUNK)
        rows = pl.ds(r0, CHUNK)
        wc = jnp.exp(cum_s[rows, :][CHUNK - 1:CHUNK, :])
        vh = v_s[rows, :]
        idx = [c * A_HEADS + h for h in heads]
        s0 = [state_ref[h] for h in heads]
        ars = [lax.dot_general(ar_s[idx[h]], s0[h].astype(BF16), _NT,
                               preferred_element_type=F32) for h in heads]
        u = [jnp.dot(tinv_s[idx[h]], (ars[h][0:CHUNK] + x0_s[idx[h]]).astype(BF16),
                     preferred_element_type=F32) for h in heads]
        y = [ars[h][CHUNK:] + y0_s[idx[h]]
             + jnp.dot(mrb_s[idx[h]], u[h].astype(BF16), preferred_element_type=F32)
             for h in heads]
        for h in heads:
            uv = jnp.concatenate([u[h], vh[:, hsl[h]]], axis=0).astype(BF16)
            state_ref[h] = s0[h] * wc[:, hsl[h]] + lax.dot_general(
                uv, bke_s[idx[h]], _TN, preferred_element_type=F32)
        for h in heads:
            mean = jnp.mean(y[h], axis=-1, keepdims=True)
            yc = y[h] - mean
            var = jnp.mean(yc * yc, axis=-1, keepdims=True)
            y_s[rows, hsl[h]] = yc * lax.rsqrt(var + RWKV_GN_EPS)
        return carry

    lax.fori_loop(0, n_chunks, phase_b, 0)

    out = (y_s[...] * gnw_ref[...] + gnb_ref[...] + bonus_s[...]) * g_s[...]
    o_ref[...] = out.astype(o_ref.dtype)


def _rwkv(p, nb, seq, tb, mu, w0, w2, a0, a2, g2, k_k, k_a, r_k, gn_w, gn_b):
    n = p.shape[0]
    nt = seq // tb
    row2 = lambda t: t.reshape(1, -1).astype(F32)
    vec_spec = lambda width: pl.BlockSpec((1, width), lambda b, t: (0, 0))
    full = lambda arr: pl.BlockSpec(arr.shape, lambda b, t: (0, 0))
    w2b, a2b, g2b = w2.astype(BF16), a2.astype(BF16), g2.astype(BF16)
    scr = lambda: pltpu.VMEM((tb, WIDTH), F32)
    nch = (tb // CHUNK) * A_HEADS
    return pl.pallas_call(
        _rwkv_kernel,
        grid=(nb, nt),
        in_specs=[pl.BlockSpec((tb, 3 * WIDTH), lambda b, t: (b * nt + t, 0)),
                  pl.BlockSpec((tb, A_LORA), lambda b, t: (b * nt + t, LORA_OFF // A_LORA)),
                  vec_spec(3 * WIDTH), vec_spec(A_LORA), vec_spec(WIDTH), full(w2b),
                  vec_spec(WIDTH), full(a2b), full(g2b), vec_spec(WIDTH), vec_spec(WIDTH),
                  vec_spec(WIDTH), vec_spec(WIDTH), vec_spec(WIDTH)],
        out_specs=pl.BlockSpec((tb, WIDTH), lambda b, t: (b * nt + t, 0)),
        out_shape=jax.ShapeDtypeStruct((n, WIDTH), BF16),
        scratch_shapes=[pltpu.VMEM((A_HEADS, HEAD_DIM, HEAD_DIM), F32),
                        pltpu.VMEM((1, 3 * WIDTH), F32),
                        pltpu.VMEM((1, A_LORA), F32)] + [scr() for _ in range(10)]
        + [pltpu.VMEM((nch, CHUNK, CHUNK), F32), pltpu.VMEM((nch, CHUNK, CHUNK), F32),
           pltpu.VMEM((nch, CHUNK, CHUNK), BF16), pltpu.VMEM((nch, CHUNK, CHUNK), BF16),
           pltpu.VMEM((nch, 2 * CHUNK, HEAD_DIM), BF16),
           pltpu.VMEM((nch, 2 * CHUNK, HEAD_DIM), BF16)],
        compiler_params=_params(("parallel", "arbitrary")),
        name="rwkv",
    )(p, p, row2(mu[:3 * WIDTH]), row2(mu[3 * WIDTH:]), row2(w0), w2b, row2(a0), a2b, g2b,
      row2(k_k), row2(k_a), row2(r_k), row2(gn_w), row2(gn_b))


def _band_kernel(q_ref, kp_ref, kc_ref, vp_ref, vc_ref, bias_ref, o_ref, k_s, vt_s):
    tq = q_ref.shape[0]
    i = pl.program_id(1)
    pair = 2 * CHUNK
    nkeys = B_BAND + CHUNK
    hsl = [slice(h * HEAD_DIM, (h + 1) * HEAD_DIM) for h in range(B_HEADS)]

    pad_rows = vt_s.shape[1] - HEAD_DIM
    rid = lax.broadcasted_iota(jnp.int32, (pad_rows, 2 * tq), 0)
    ones_rows = jnp.where(rid == 0, 1.0, 0.0).astype(BF16)
    for half, (kr, vr) in enumerate(((kp_ref, vp_ref), (kc_ref, vc_ref))):
        rows = slice(half * tq, (half + 1) * tq)
        vt = vr[...].astype(F32).T
        for h in range(B_HEADS):
            k_s[h, rows, :] = kr[:, hsl[h]]
            vt_s[h, 0:HEAD_DIM, rows] = vt[hsl[h], :].astype(BF16)
    for h in range(B_HEADS):
        vt_s[h, HEAD_DIM:, :] = ones_rows

    q = q_ref[...] * (HEAD_DIM ** -0.5)
    krow = lax.broadcasted_iota(jnp.int32, (nkeys, pair), 0)
    for pi in range(tq // pair):
        off = pi * pair
        valid = jnp.logical_or(i > 0, krow + off >= tq)
        heads = range(B_HEADS)
        s = [lax.dot_general(k_s[h, off:off + nkeys, :], q[off:off + pair, hsl[h]], _NT,
                             preferred_element_type=F32) for h in heads]
        s = [jnp.where(valid, s[h] + bias_ref[h], NEG_INF) for h in heads]
        m = [jnp.max(s[h], axis=0, keepdims=True) for h in heads]
        e = [jnp.exp(s[h] - m[h]).astype(BF16) for h in heads]
        acc = [jnp.dot(vt_s[h, :, off:off + nkeys], e[h], preferred_element_type=F32)
               for h in heads]
        outs = [a[0:HEAD_DIM] / a[HEAD_DIM:HEAD_DIM + 1] for a in acc]
        o_ref[off:off + pair, :] = jnp.concatenate(outs, axis=0).T.astype(o_ref.dtype)


def _band(p, nb, seq, tq, bias):
    n = p.shape[0]
    nt = seq // tq
    qb, kb, vb = B_OFF // WIDTH, B_OFF // WIDTH + 1, B_OFF // WIDTH + 2
    cur = lambda cb: pl.BlockSpec((tq, WIDTH), lambda b, t: (b * nt + t, cb))
    prv = lambda cb: pl.BlockSpec((tq, WIDTH), lambda b, t: (b * nt + jnp.maximum(t - 1, 0), cb))
    return pl.pallas_call(
        _band_kernel,
        grid=(nb, nt),
        in_specs=[cur(qb), prv(kb), cur(kb), prv(vb), cur(vb),
                  pl.BlockSpec(bias.shape, lambda b, t: (0, 0, 0))],
        out_specs=pl.BlockSpec((tq, WIDTH), lambda b, t: (b * nt + t, 0)),
        out_shape=jax.ShapeDtypeStruct((n, WIDTH), BF16),
        scratch_shapes=[pltpu.VMEM((B_HEADS, 2 * tq, HEAD_DIM), BF16),
                        pltpu.VMEM((B_HEADS, HEAD_DIM + BF16_SUBLANES, 2 * tq), BF16)],
        compiler_params=_params(("parallel", "arbitrary")),
        name="band",
    )(p, p, p, p, p, bias)


def _diff_kernel(q_ref, k_ref, v_ref, bd_ref, bp_ref,
                 lq1_ref, lk1_ref, lq2_ref, lk2_ref, sub_ref, o_ref,
                 k_s, vt_s, m_s, acc_s, *, lam_init):
    tq = q_ref.shape[0]
    seq = k_ref.shape[0]
    hw = 2 * HEAD_DIM
    i = pl.program_id(2)

    @pl.when(i == 0)
    def _():
        for comp in range(2):
            k_s[comp] = k_ref[:, comp * HEAD_DIM:(comp + 1) * HEAD_DIM]
        pad_rows = vt_s.shape[1] - hw
        rid = lax.broadcasted_iota(jnp.int32, (pad_rows, tq), 0)
        ones_row = jnp.where(rid == 0, 1.0, 0.0).astype(BF16)
        for j in range(seq // tq):
            vt = v_ref[j * tq:(j + 1) * tq, :].astype(F32).T.astype(BF16)
            vt_s[j] = jnp.concatenate([vt, ones_row], axis=0)

    q = q_ref[...] * (HEAD_DIM ** -0.5)
    qs = (q[:, 0:HEAD_DIM], q[:, HEAD_DIM:])

    def scores(j, comp, bias, width=1):
        r0 = pl.multiple_of(j * tq, tq)
        kc = k_s[comp, pl.ds(r0, width * tq), :]
        s = lax.dot_general(kc, qs[comp], _NT, preferred_element_type=F32)
        return s if bias is None else s + bias

    comps = range(2)

    def first_tile(j, bias):
        s = [scores(j, c, bias) for c in comps]
        m = [jnp.max(s[c], axis=0, keepdims=True) for c in comps]
        e = [jnp.exp(s[c] - m[c]).astype(BF16) for c in comps]
        vt = vt_s[j]
        for c in comps:
            m_s[c] = m[c]
            acc_s[c] = jnp.dot(vt, e[c], preferred_element_type=F32)

    def next_tile(j, bias, width=1):
        s = [scores(j, c, bias, width) for c in comps]
        m_old = [m_s[c] for c in comps]
        m_new = [jnp.maximum(m_old[c], jnp.max(s[c], axis=0, keepdims=True)) for c in comps]
        e = [jnp.exp(s[c] - m_new[c]).astype(BF16) for c in comps]
        alpha = [jnp.exp(m_old[c] - m_new[c]) for c in comps]
        pv = [jnp.dot(vt_s[j], e[c][0:tq], preferred_element_type=F32) for c in comps]
        for t in range(1, width):
            pv = [pv[c] + jnp.dot(vt_s[j + t], e[c][t * tq:(t + 1) * tq],
                                  preferred_element_type=F32) for c in comps]
        for c in comps:
            m_s[c] = m_new[c]
            acc_s[c] = alpha[c] * acc_s[c] + pv[c]

    first_tile(i, bd_ref[...])

    @pl.when(i >= 1)
    def _():
        next_tile(i - 1, bp_ref[...])

    n_far = jnp.maximum(i - 1, 0)

    def far_pair(jj, carry):
        next_tile(2 * jj, None, width=2)
        return carry

    lax.fori_loop(0, lax.shift_right_logical(n_far, 1), far_pair, 0)

    @pl.when(lax.bitwise_and(n_far, 1) == 1)
    def _():
        next_tile(n_far - 1, None)

    lam = (jnp.exp(jnp.sum(lq1_ref[...] * lk1_ref[...], axis=-1, keepdims=True))
           - jnp.exp(jnp.sum(lq2_ref[...] * lk2_ref[...], axis=-1, keepdims=True))
           + lam_init)
    a1, a2 = acc_s[0], acc_s[1]
    out = a1[0:hw] / a1[hw:hw + 1] - lam * (a2[0:hw] / a2[hw:hw + 1])
    out = out * lax.rsqrt(jnp.mean(out * out, axis=0, keepdims=True) + SUBLN_EPS)
    out = out * sub_ref[...] * (1.0 - lam_init)
    o_ref[...] = out.T.astype(o_ref.dtype)


def _diff(p, nb, seq, tq, bias_diag, bias_prev, lq1, lk1, lq2, lk2, subw, lam_init):
    n = p.shape[0]
    nt = seq // tq
    hw = 2 * HEAD_DIM
    vrows = hw + BF16_SUBLANES
    qb, kb, vb = C_OFF // hw, (C_OFF + WIDTH) // hw, (C_OFF + 2 * WIDTH) // hw
    row2 = lambda t: t.reshape(1, -1).astype(F32)
    vec = lambda width: pl.BlockSpec((1, width), lambda b, h, t: (0, 0))
    return pl.pallas_call(
        functools.partial(_diff_kernel, lam_init=lam_init),
        grid=(nb, C_HEADS, nt),
        in_specs=[pl.BlockSpec((tq, hw), lambda b, h, t: (b * nt + t, qb + h)),
                  pl.BlockSpec((seq, hw), lambda b, h, t: (b, kb + h)),
                  pl.BlockSpec((seq, hw), lambda b, h, t: (b, vb + h)),
                  pl.BlockSpec((None, tq, tq), lambda b, h, t: (h, 0, 0)),
                  pl.BlockSpec((None, tq, tq), lambda b, h, t: (h, 0, 0)),
                  vec(HEAD_DIM), vec(HEAD_DIM), vec(HEAD_DIM), vec(HEAD_DIM),
                  pl.BlockSpec((hw, 1), lambda b, h, t: (0, 0))],
        out_specs=pl.BlockSpec((tq, hw), lambda b, h, t: (b * nt + t, h)),
        out_shape=jax.ShapeDtypeStruct((n, WIDTH), BF16),
        scratch_shapes=[pltpu.VMEM((2, seq, HEAD_DIM), BF16),
                        pltpu.VMEM((nt, vrows, tq), BF16),
                        pltpu.VMEM((2, 1, tq), F32),
                        pltpu.VMEM((2, vrows, tq), F32)],
        compiler_params=_params(("parallel", "parallel", "arbitrary")),
        name="diffattn",
    )(p, p, p, bias_diag, bias_prev, row2(lq1), row2(lk1), row2(lq2),
      row2(lk2), subw.reshape(-1, 1).astype(F32))


def _t5_buckets(rel):
    nb = T5_BUCKETS // 2
    max_exact = nb // 2
    ret = (rel > 0).astype(jnp.int32) * nb
    n = jnp.abs(rel)
    nf = jnp.maximum(n, 1).astype(jnp.float32)
    large = max_exact + (jnp.log(nf / max_exact) / math.log(T5_MAX_DIST / max_exact)
                         * (nb - max_exact)).astype(jnp.int32)
    large = jnp.minimum(large, nb - 1)
    return ret + jnp.where(n < max_exact, n, large)


def _toeplitz(vec, rows, cols):
    h, period = vec.shape
    flat = jnp.tile(vec, (1, rows))[:, :rows * (period - 1)]
    return flat.reshape(h, rows, period - 1)[:, :, :cols]


def _t5_tiles(t5_table, tq):
    tab = t5_table.astype(F32)
    period = 2 * tq
    m = jnp.arange(period)
    qk = jnp.where(m < tq, m, m - period)
    far = tab[_t5_buckets(jnp.int32(-2 * tq))]
    vec_d = tab[_t5_buckets(-qk)].T - far[:, None]
    vec_p = tab[_t5_buckets(-tq - qk)].T - far[:, None]
    diag = _toeplitz(vec_d, tq, tq)
    prev = _toeplitz(vec_p, tq, tq)
    qi = jnp.arange(tq)[None, :]
    ki = jnp.arange(tq)[:, None]
    diag = jnp.where((ki // CHUNK) <= (qi // CHUNK), diag, NEG_INF)
    return diag, prev


def _band_bias(rel_bias, keys, queries):
    period = keys + queries
    m = jnp.arange(period)
    qk = jnp.where(m < queries, m, m - period)
    dist = B_LEFT_CHUNKS * CHUNK + qk
    vec = rel_bias[:, jnp.clip(dist, -REL_CLIP, REL_CLIP) + REL_CLIP].astype(F32)
    bias = _toeplitz(vec, keys, queries)
    qi = jnp.arange(queries)[None, :]
    ki = jnp.arange(keys)[:, None]
    in_window = jnp.where(qi < CHUNK, ki < B_BAND, ki >= CHUNK)
    return jnp.where(in_window, bias, NEG_INF)


def _outproj_kernel(ya_ref, yb_ref, yc_ref, pb_ref, pc_ref, ph_ref, pcp_ref, php_ref,
                    cw_ref, w_ref, x_ref, g_ref, rwh_ref, rwl_ref, rb_ref, o_ref, route_ref,
                    *, tiles_per_seq):
    tm = x_ref.shape[0]
    i = pl.program_id(0)
    u = pc_ref[...].astype(F32) * ph_ref[...].astype(F32)
    up = pcp_ref[...].astype(F32) * php_ref[...].astype(F32)
    up = jnp.where(i % tiles_per_seq == 0, 0.0, up)
    row = lax.broadcasted_iota(jnp.int32, (tm, 1), 0)
    s1 = jnp.where(row == 0, up[7:8, :], pltpu.roll(u, 1, axis=0))
    s2 = pltpu.roll(u, 2, axis=0)
    s2 = jnp.where(row == 0, up[6:7, :], jnp.where(row == 1, up[7:8, :], s2))
    cw = cw_ref[...]
    yd = pb_ref[...].astype(F32) * (cw[0:1, :] * s2 + cw[1:2, :] * s1 + cw[2:3, :] * u)
    acc = jnp.dot(ya_ref[...], w_ref[0:WIDTH, :], preferred_element_type=F32)
    acc += jnp.dot(yb_ref[...], w_ref[WIDTH:2 * WIDTH, :], preferred_element_type=F32)
    acc += jnp.dot(yc_ref[...], w_ref[2 * WIDTH:3 * WIDTH, :], preferred_element_type=F32)
    acc += jnp.dot(yd.astype(BF16), w_ref[3 * WIDTH:, :], preferred_element_type=F32)
    x_new = x_ref[...] + acc
    o_ref[...] = x_new
    route_ref[...] = _route(x_new, g_ref[...], rwh_ref[...], rwl_ref[...], rb_ref[...])


def _outproj(ya, yb, yc, p, conv_w, w_bf16, x2d, gain_ffn, wr_hi, wr_lo, br, seq, tm):
    n, d = x2d.shape
    db = D_OFF // WIDTH
    r8 = tm // 8
    ycur = pl.BlockSpec((tm, WIDTH), lambda i: (i, 0))
    pcur = lambda cb: pl.BlockSpec((tm, WIDTH), lambda i: (i, cb))
    pprev = lambda cb: pl.BlockSpec((8, WIDTH), lambda i: (jnp.maximum(i * r8 - 1, 0), cb))
    return pl.pallas_call(
        functools.partial(_outproj_kernel, tiles_per_seq=seq // tm),
        grid=(n // tm,),
        in_specs=[ycur, ycur, ycur, pcur(db), pcur(db + 1), pcur(db + 2),
                  pprev(db + 1), pprev(db + 2),
                  pl.BlockSpec(conv_w.shape, lambda i: (0, 0)),
                  pl.BlockSpec(w_bf16.shape, lambda i: (0, 0)),
                  pl.BlockSpec((tm, d), lambda i: (i, 0)),
                  pl.BlockSpec((1, d), lambda i: (0, 0)),
                  pl.BlockSpec((d, LANES), lambda i: (0, 0)),
                  pl.BlockSpec((d, LANES), lambda i: (0, 0)),
                  pl.BlockSpec((1, LANES), lambda i: (0, 0))],
        out_specs=[pl.BlockSpec((tm, d), lambda i: (i, 0)),
                   pl.BlockSpec((tm, LANES), lambda i: (i, 0))],
        out_shape=[jax.ShapeDtypeStruct((n, d), F32),
                   jax.ShapeDtypeStruct((n, LANES), F32)],
        compiler_params=_params(("parallel",)),
        name="outproj",
    )(ya, yb, yc, p, p, p, p, p, conv_w.astype(F32), w_bf16, x2d, gain_ffn, wr_hi, wr_lo, br)


def _route(x, gain, w_hi, w_lo, bias):
    ms = jnp.mean(x * x, axis=-1, keepdims=True)
    xn = x * lax.rsqrt(ms + RMS_EPS) * gain
    xh = xn.astype(BF16)
    xl = (xn - xh.astype(F32)).astype(BF16)
    logits = (jnp.dot(xh, w_hi, preferred_element_type=F32)
              + jnp.dot(xh, w_lo, preferred_element_type=F32)
              + jnp.dot(xl, w_hi, preferred_element_type=F32)) + bias
    tm = x.shape[0]
    lane_i = lax.broadcasted_iota(jnp.int32, (tm, LANES), 1)
    lane = lane_i.astype(F32)
    lane_grp = lax.shift_right_logical(lane_i, 3).astype(F32)
    is_g = (lane_i >= N_EXPERTS) & (lane_i < N_EXPERTS + N_GROUPS)
    gl = jnp.where(is_g, logits, NEG_INF)
    gmax = jnp.max(gl, axis=-1, keepdims=True)
    gsum = jnp.sum(jnp.where(is_g, jnp.exp(gl - gmax), 0.0), axis=-1, keepdims=True)
    g_w = 1.0 / gsum
    g_idx = jnp.min(jnp.where(is_g & (gl == gmax), lane, float(LANES)), axis=-1,
                    keepdims=True) - float(N_EXPERTS)
    in_grp = (lane_i < N_EXPERTS) & (lane_grp == g_idx)
    el = jnp.where(in_grp, logits, NEG_INF)
    emax = jnp.max(el, axis=-1, keepdims=True)
    ee = jnp.where(in_grp, jnp.exp(el - emax), 0.0)
    ep = ee / jnp.sum(ee, axis=-1, keepdims=True)
    p1 = jnp.max(ep, axis=-1, keepdims=True)
    i1 = jnp.min(jnp.where(in_grp & (ep == p1), lane, float(LANES)), axis=-1, keepdims=True)
    rest = in_grp & (lane != i1)
    ep2 = jnp.where(rest, ep, -1.0)
    p2 = jnp.max(ep2, axis=-1, keepdims=True)
    i2 = jnp.min(jnp.where(rest & (ep2 == p2), lane, float(LANES)), axis=-1, keepdims=True)
    tot = p1 + p2
    return (jnp.where(lane_i == ROUTE_ID1, i1, 0.0)
            + jnp.where(lane_i == ROUTE_ID2, i2, 0.0)
            + jnp.where(lane_i == ROUTE_W1, g_w * p1 / tot, 0.0)
            + jnp.where(lane_i == ROUTE_W2, g_w * p2 / tot, 0.0))


def _gather_pipeline(step, n_steps, n_live, idx_hbm, src_hbm, idx_smem, buf, isem, gsem):
    groups = buf.shape[1]

    def idx_copy(j, slot):
        return pltpu.make_async_copy(idx_hbm.at[pl.ds(j * IDX_TILE, IDX_TILE)],
                                     idx_smem.at[pl.ds(slot * IDX_TILE, IDX_TILE)],
                                     isem.at[slot])

    def row_copy(token, g, sub, slot):
        src = src_hbm.at[lax.shift_right_logical(token, 3),
                         pl.ds(lax.bitwise_and(token, F32_SUBLANES - 1), 1)]
        return pltpu.make_async_copy(src, buf.at[slot, g, pl.ds(sub, 1)], gsem.at[slot])

    def start_rows(slot):
        def body(g, carry):
            base = slot * IDX_TILE + g * F32_SUBLANES
            for sub in range(F32_SUBLANES):
                row_copy(idx_smem[base + sub], g, sub, slot).start(priority=sub % 2)
            return carry
        lax.fori_loop(0, groups, body, 0)

    def wait_rows(slot):
        def body(g, carry):
            for sub in range(F32_SUBLANES):
                row_copy(0, g, sub, slot).wait()
            return carry
        lax.fori_loop(0, groups, body, 0)

    cur = lax.rem(step, 2)
    nxt = 1 - cur

    @pl.when(step == 0)
    def _():
        first = idx_copy(0, 0)
        first.start()
        first.wait()
        start_rows(0)
        if n_steps > 1:
            idx_copy(1, 1).start()

    @pl.when(step + 1 < n_steps)
    def _():
        idx_copy(step + 1, nxt).wait()

        @pl.when(step + 1 < n_live)
        def _():
            start_rows(nxt)

    @pl.when(step + 2 < n_steps)
    def _():
        idx_copy(step + 2, cur).start()

    @pl.when(step < n_live)
    def _():
        wait_rows(cur)

    return cur


def _dispatch_kernel(lastg_ref, hast_ref, nv_ref, idx_hbm, x_ref, xs_hbm,
                     idx_smem, zbuf, isem, ssem, zsem, *, n_steps):
    j = pl.program_id(0)
    groups = x_ref.shape[0]
    tm = groups * F32_SUBLANES
    tile_groups = zbuf.shape[0]
    n_tiles = xs_hbm.shape[0] // tile_groups
    cur = lax.rem(j, 2)

    def idx_copy(step, slot):
        return pltpu.make_async_copy(idx_hbm.at[pl.ds(step * IDX_TILE, IDX_TILE)],
                                     idx_smem.at[pl.ds(slot * IDX_TILE, IDX_TILE)],
                                     isem.at[slot])

    def zero_copy(group):
        return pltpu.make_async_copy(zbuf, xs_hbm.at[pl.ds(group, tile_groups)], zsem.at[0])

    @pl.when(j == 0)
    def _():
        idx_copy(0, 0).start()
        zbuf[...] = jnp.zeros_like(zbuf)
        n_live = nv_ref[0]
        for e in range(N_EXPERTS):
            @pl.when(hast_ref[e] > 0)
            def _():
                zero_copy(lastg_ref[e]).start()

            @pl.when(n_live + e < n_tiles)
            def _():
                zero_copy((n_live + e) * tile_groups).start()
        for e in range(N_EXPERTS):
            @pl.when(hast_ref[e] > 0)
            def _():
                zero_copy(0).wait()

            @pl.when(n_live + e < n_tiles)
            def _():
                zero_copy(0).wait()

    idx_copy(j, cur).wait()

    @pl.when(j + 1 < n_steps)
    def _():
        idx_copy(j + 1, 1 - cur).start()

    def row_copy(g, sub, dest):
        dst = xs_hbm.at[lax.shift_right_logical(dest, 3),
                        pl.ds(lax.bitwise_and(dest, F32_SUBLANES - 1), 1)]
        return pltpu.make_async_copy(x_ref.at[g, pl.ds(sub, 1)], dst, ssem.at[0])

    def start_rows(g, carry):
        base = cur * IDX_TILE + g * F32_SUBLANES
        for sub in range(F32_SUBLANES):
            row_copy(g, sub, idx_smem[base + sub]).start(priority=0)
            row_copy(g, sub, idx_smem[base + tm + sub]).start(priority=1)
        return carry

    def wait_rows(g, carry):
        for sub in range(2 * F32_SUBLANES):
            row_copy(g, sub % F32_SUBLANES, 0).wait()
        return carry

    lax.fori_loop(0, groups, start_rows, 0)
    lax.fori_loop(0, groups, wait_rows, 0)


def _dispatch(x2d, pos_idx, last_group, has_tile, n_live, n_rows, tm, tm_e):
    n, d = x2d.shape
    n_steps = n // tm
    g = F32_SUBLANES
    return pl.pallas_call(
        functools.partial(_dispatch_kernel, n_steps=n_steps),
        grid_spec=pltpu.PrefetchScalarGridSpec(
            num_scalar_prefetch=3,
            grid=(n_steps,),
            in_specs=[pl.BlockSpec(memory_space=pl.ANY),
                      pl.BlockSpec((tm // g, g, d), lambda j, lg, ht, nv: (j, 0, 0))],
            out_specs=pl.BlockSpec(memory_space=pl.ANY),
            scratch_shapes=[pltpu.SMEM((2 * IDX_TILE,), jnp.int32),
                            pltpu.VMEM((tm_e // g, g, d), F32),
                            pltpu.SemaphoreType.DMA((2,)),
                            pltpu.SemaphoreType.DMA((1,)),
                            pltpu.SemaphoreType.DMA((1,))]),
        out_shape=jax.ShapeDtypeStruct((n_rows // g, g, d), F32),
        compiler_params=_params(("arbitrary",)),
        name="dispatch",
    )(last_group, has_tile, n_live, pos_idx, x2d.reshape(n // g, g, d))


def _experts_kernel(te_ref, nv_ref, x_ref, g_ref, wg_ref, wu_ref, wd_ref, y_ref,
                    wg_s, wu_s, wd_s):
    j = pl.program_id(0)
    n_live = nv_ref[0]

    @pl.when(jnp.logical_or(j == 0, te_ref[j] != te_ref[jnp.maximum(j - 1, 0)]))
    def _():
        wg_s[...] = wg_ref[...].astype(BF16)
        wu_s[...] = wu_ref[...].astype(BF16)
        wd_s[...] = wd_ref[...].astype(BF16)

    @pl.when(j < n_live)
    def _():
        x = x_ref[...]
        ms = jnp.mean(x * x, axis=-1, keepdims=True)
        xn = (x * lax.rsqrt(ms + RMS_EPS) * g_ref[...]).astype(BF16)
        hg = jnp.dot(xn, wg_s[...], preferred_element_type=F32)
        hu = jnp.dot(xn, wu_s[...], preferred_element_type=F32)
        h = hg * _sigmoid(hg) * hu
        y_ref[...] = jnp.dot(h.astype(BF16), wd_s[...], preferred_element_type=F32)

    @pl.when(j >= n_live)
    def _():
        y_ref[...] = jnp.zeros_like(y_ref)


def _experts(xs, gain, tile_expert, n_live, wg, wu, wd, layer, tm):
    n_rows, d = xs.shape
    hid = wg.shape[-1]
    n_steps = tile_expert.shape[0]
    live = lambda j, nv: jnp.minimum(j, nv[0] - 1)
    return pl.pallas_call(
        _experts_kernel,
        grid_spec=pltpu.PrefetchScalarGridSpec(
            num_scalar_prefetch=2,
            grid=(n_steps,),
            in_specs=[pl.BlockSpec((tm, d), lambda j, te, nv: (live(j, nv), 0)),
                      pl.BlockSpec((1, d), lambda j, te, nv: (0, 0)),
                      pl.BlockSpec((None, None, d, hid), lambda j, te, nv: (layer, te[j], 0, 0)),
                      pl.BlockSpec((None, None, d, hid), lambda j, te, nv: (layer, te[j], 0, 0)),
                      pl.BlockSpec((None, None, hid, d), lambda j, te, nv: (layer, te[j], 0, 0))],
            out_specs=pl.BlockSpec((tm, d), lambda j, te, nv: (j, 0)),
            scratch_shapes=[pltpu.VMEM((d, hid), BF16), pltpu.VMEM((d, hid), BF16),
                            pltpu.VMEM((hid, d), BF16)]),
        out_shape=jax.ShapeDtypeStruct((n_rows, d), F32),
        compiler_params=_params(("arbitrary",)),
        name="experts",
    )(tile_expert, n_live, xs, gain, wg, wu, wd)


def _combine_kernel(idx_hbm, y_hbm, x_ref, route_ref, gf_ref, o_ref,
                    idx_smem, ybuf, isem, gsem, *, n_steps, final_norm):
    j = pl.program_id(0)
    tm = x_ref.shape[0]
    cur = _gather_pipeline(j, n_steps, n_steps, idx_hbm, y_hbm, idx_smem, ybuf, isem, gsem)
    route = route_ref[...]
    w1 = route[:, ROUTE_W1:ROUTE_W1 + 1]
    w2 = route[:, ROUTE_W2:ROUTE_W2 + 1]
    yb = ybuf[cur].reshape(2 * tm, x_ref.shape[1])
    y = x_ref[...] + w1 * yb[0:tm] + w2 * yb[tm:]
    if final_norm:
        ms = jnp.mean(y * y, axis=-1, keepdims=True)
        y = y * lax.rsqrt(ms + RMS_EPS) * gf_ref[...]
    o_ref[...] = y


def _combine(x2d, y_rows, pos_idx, route, gain_final, final_norm, tm):
    n, d = x2d.shape
    n_steps = n // tm
    return pl.pallas_call(
        functools.partial(_combine_kernel, n_steps=n_steps, final_norm=final_norm),
        grid=(n_steps,),
        in_specs=[pl.BlockSpec(memory_space=pl.ANY),
                  pl.BlockSpec(memory_space=pl.ANY),
                  pl.BlockSpec((tm, d), lambda j: (j, 0)),
                  pl.BlockSpec((tm, LANES), lambda j: (j, 0)),
                  pl.BlockSpec((1, d), lambda j: (0, 0))],
        out_specs=pl.BlockSpec((tm, d), lambda j: (j, 0)),
        out_shape=jax.ShapeDtypeStruct((n, d), F32),
        scratch_shapes=[pltpu.SMEM((2 * IDX_TILE,), jnp.int32),
                        pltpu.VMEM((2, 2 * tm // F32_SUBLANES, F32_SUBLANES, d), F32),
                        pltpu.SemaphoreType.DMA((2,)),
                        pltpu.SemaphoreType.DMA((2,))],
        compiler_params=_params(("arbitrary",)),
        name="combine",
    )(pos_idx, y_rows.reshape(-1, F32_SUBLANES, d), x2d, route, gain_final)


def _dispatch_plan(e1, e2, tm_e, tm_c):
    n = e1.shape[0]
    n_tiles = (2 * n) // tm_e + N_EXPERTS
    e = jnp.concatenate([e1, e2])
    onehot = (e[:, None] == jnp.arange(N_EXPERTS, dtype=jnp.int32)[None, :]).astype(jnp.int32)
    csum = jnp.cumsum(onehot, axis=0)
    rank = jnp.sum((csum - onehot) * onehot, axis=1)
    counts = csum[-1]
    padded = ((counts + tm_e - 1) // tm_e) * tm_e
    ends = jnp.cumsum(padded)
    dest = jnp.sum((ends - padded)[None, :] * onehot, axis=1) + rank
    pos = jnp.concatenate([dest[:n].reshape(n // tm_c, tm_c), dest[n:].reshape(n // tm_c, tm_c)],
                          axis=1)
    pos = jnp.pad(pos, ((0, 0), (0, IDX_TILE - 2 * tm_c))).reshape(-1)
    tile_start = jnp.arange(n_tiles, dtype=jnp.int32) * tm_e
    tile_expert = jnp.minimum(jnp.sum((tile_start[:, None] >= ends[None, :]).astype(jnp.int32),
                                      axis=1), N_EXPERTS - 1)
    n_live = (ends[-1] // tm_e).reshape(1)
    last_group = jnp.maximum(ends - tm_e, 0) // F32_SUBLANES
    has_tile = (padded > 0).astype(jnp.int32)
    return (pos, tile_expert.astype(jnp.int32), n_live.astype(jnp.int32),
            last_group.astype(jnp.int32), has_tile)


def _pick(total, pref):
    t = min(pref, total)
    while total % t:
        t //= 2
    return t


def kernel(x, w_in, w_out, norm_mix, norm_ffn, norm_final, rwkv_mu, rwkv_w0, rwkv_w2, rwkv_a0, rwkv_a2, rwkv_g2, rwkv_k_k, rwkv_k_a, rwkv_r_k, rwkv_gn_w, rwkv_gn_b, band_rel_bias, t5_rel_bias, diff_lambda_q1, diff_lambda_k1, diff_lambda_q2, diff_lambda_k2, diff_subln_w, conv_w, router_group_w, router_group_b, router_expert_w, router_expert_b, expert_w_gate, expert_w_up, expert_w_down):
    nb, seq, d = x.shape
    n = nb * seq
    depth = w_in.shape[0]
    x2d = x.reshape(n, d)

    tm_proj = _pick(n, 1024)
    tm_moe = _pick(n, 512)
    tm_exp = _pick(n, 256)
    t_rwkv = _pick(seq, 256)
    t_band = _pick(seq, 512)
    t_diff = _pick(seq, 512)
    row2 = lambda t: t.reshape(1, -1).astype(F32)

    t5_diag, t5_prev = _t5_tiles(t5_rel_bias, t_diff)

    for l in range(depth):
        wl = w_in[l]
        w_perm = jnp.concatenate([wl[:, :3 * WIDTH], wl[:, A_COLS:], wl[:, 3 * WIDTH:A_COLS]],
                                 axis=1).astype(BF16)
        p = _inproj(x2d, row2(norm_mix[l]), w_perm, tm_proj, 1280)

        ya = _rwkv(p, nb, seq, t_rwkv, rwkv_mu[l], rwkv_w0[l], rwkv_w2[l], rwkv_a0[l],
                   rwkv_a2[l], rwkv_g2[l], rwkv_k_k[l], rwkv_k_a[l], rwkv_r_k[l],
                   rwkv_gn_w[l], rwkv_gn_b[l])
        yb = _band(p, nb, seq, t_band,
                   _band_bias(band_rel_bias[l], B_BAND + CHUNK, 2 * CHUNK))
        lam_init = 0.8 - 0.6 * math.exp(-0.3 * l)
        yc = _diff(p, nb, seq, t_diff, t5_diag, t5_prev,
                   diff_lambda_q1[l], diff_lambda_k1[l], diff_lambda_q2[l],
                   diff_lambda_k2[l], diff_subln_w[l], lam_init)
        wr = jnp.concatenate([router_expert_w[l], router_group_w[l]], axis=1).astype(F32)
        wr = jnp.pad(wr, ((0, 0), (0, LANES - wr.shape[1])))
        wr_hi = wr.astype(BF16)
        wr_lo = (wr - wr_hi.astype(F32)).astype(BF16)
        br = jnp.concatenate([router_expert_b[l], router_group_b[l]]).astype(F32)
        br = jnp.pad(br, (0, LANES - br.shape[0])).reshape(1, LANES)
        x2d, route = _outproj(ya, yb, yc, p, conv_w[l], w_out[l].astype(BF16), x2d,
                              row2(norm_ffn[l]), wr_hi, wr_lo, br, seq, tm_moe)
        pos_idx, tile_expert, n_live, last_group, has_tile = _dispatch_plan(
            route[:, ROUTE_ID1].astype(jnp.int32), route[:, ROUTE_ID2].astype(jnp.int32),
            tm_exp, tm_moe)
        n_rows = tile_expert.shape[0] * tm_exp
        xs = _dispatch(x2d, pos_idx, last_group, has_tile, n_live, n_rows, tm_moe, tm_exp)
        y_rows = _experts(xs.reshape(n_rows, d), row2(norm_ffn[l]), tile_expert, n_live,
                          expert_w_gate, expert_w_up, expert_w_down, l, tm_exp)
        x2d = _combine(x2d, y_rows, pos_idx, route, row2(norm_final), l == depth - 1, tm_moe)
    return x2d.reshape(nb, seq, d)
```

```python
import functools
import math

import jax
import jax.numpy as jnp
from jax import lax
from jax.experimental import pallas as pl
from jax.experimental.pallas import tpu as pltpu

F32 = jnp.float32
BF16 = jnp.bfloat16

DEPTH = 2
CHUNK = 64
CHUNK_SHIFT = 6
HEAD_DIM = 64
A_HEADS = 8
B_HEADS = 8
C_HEADS = 4
WIDTH = 512
A_DECAY_LORA = 64
A_ICLR_LORA = 64
A_GATE_LORA = 128
A_LORA = A_DECAY_LORA + A_ICLR_LORA + A_GATE_LORA
A_COLS = 3 * WIDTH + A_LORA
B_OFF = 3 * WIDTH
C_OFF = B_OFF + 3 * WIDTH
D_OFF = C_OFF + 3 * WIDTH
LORA_OFF = D_OFF + 3 * WIDTH
IN_COLS = LORA_OFF + A_LORA
B_LEFT_CHUNKS = 8
B_BAND = (B_LEFT_CHUNKS + 1) * CHUNK
REL_CLIP = 128
T5_BUCKETS = 32
T5_MAX_DIST = 128
N_GROUPS = 4
EXPERTS_PER_GROUP = 8
N_EXPERTS = 32
RMS_EPS = 1e-6
RWKV_GN_EPS = 64e-5
SUBLN_EPS = 1e-5
NEG_INF = -1e30

LANES = 128
F32_SUBLANES = 8
BF16_SUBLANES = 16
IDX_TILE = 1024

ROUTE_ID1, ROUTE_ID2, ROUTE_W1, ROUTE_W2 = 0, 1, 2, 3
VMEM_LIMIT_BYTES = 56 * 1024 * 1024

_NT = (((1,), (1,)), ((), ()))
_TN = (((0,), (0,)), ((), ()))


def _params(semantics):
    return pltpu.CompilerParams(dimension_semantics=semantics,
                                vmem_limit_bytes=VMEM_LIMIT_BYTES)


def _bdot(a, b):
    return jnp.dot(a.astype(BF16), b.astype(BF16), preferred_element_type=F32)


def _sigmoid(x):
    return 1.0 / (1.0 + jnp.exp(-x))


def _inproj_kernel(x_ref, g_ref, w_ref, o_ref, xn_ref):
    @pl.when(pl.program_id(1) == 0)
    def _():
        x = x_ref[...]
        ms = jnp.mean(x * x, axis=-1, keepdims=True)
        xn_ref[...] = (x * lax.rsqrt(ms + RMS_EPS) * g_ref[...]).astype(BF16)

    o_ref[...] = jnp.dot(xn_ref[...], w_ref[...],
                         preferred_element_type=F32).astype(o_ref.dtype)


def _inproj(x2d, gain, w_bf16, tm, tn):
    n, d = x2d.shape
    cols = w_bf16.shape[1]
    return pl.pallas_call(
        _inproj_kernel,
        grid=(n // tm, cols // tn),
        in_specs=[pl.BlockSpec((tm, d), lambda i, j: (i, 0)),
                  pl.BlockSpec((1, d), lambda i, j: (0, 0)),
                  pl.BlockSpec((d, tn), lambda i, j: (0, j))],
        out_specs=pl.BlockSpec((tm, tn), lambda i, j: (i, j)),
        out_shape=jax.ShapeDtypeStruct((n, cols), BF16),
        scratch_shapes=[pltpu.VMEM((tm, d), BF16)],
        compiler_params=_params(("parallel", "arbitrary")),
        name="inproj",
    )(x2d, gain, w_bf16)


def _rwkv_kernel(p_ref, pl_ref, mu_ref, mul_ref, w0_ref, w2_ref, a0_ref, a2_ref, g2_ref,
                 kk_ref, ka_ref, rk_ref, gnw_ref, gnb_ref,
                 o_ref,
                 state_ref, prev_ref, prevl_ref, r_s, k_s, v_s, na_s, b_s, ld_s, cum_s,
                 y_s, g_s, bonus_s, x0_s, y0_s, mrb_s, tinv_s, ar_s, bke_s):
    tb = p_ref.shape[0]
    n_chunks = tb // CHUNK

    @pl.when(pl.program_id(1) == 0)
    def _():
        state_ref[...] = jnp.zeros_like(state_ref)
        prev_ref[...] = jnp.zeros_like(prev_ref)
        prevl_ref[...] = jnp.zeros_like(prevl_ref)

    row = lax.broadcasted_iota(jnp.int32, (tb, 1), 0)

    def token_shift(src_ref, last_ref, m_ref):
        pa = src_ref[...].astype(F32)
        shifted = jnp.where(row == 0, last_ref[...], pltpu.roll(pa, 1, axis=0))
        last_ref[...] = pa[tb - 1:tb, :]
        return pa + (shifted - pa) * m_ref[...]

    ps = token_shift(p_ref, prev_ref, mu_ref)
    lora = token_shift(pl_ref, prevl_ref, mul_ref)
    r = ps[:, 0:WIDTH]
    k = ps[:, WIDTH:2 * WIDTH]
    v = ps[:, 2 * WIDTH:3 * WIDTH]
    w_lo = lora[:, 0:A_DECAY_LORA]
    a_lo = lora[:, A_DECAY_LORA:A_DECAY_LORA + A_ICLR_LORA]
    g_lo = lora[:, A_DECAY_LORA + A_ICLR_LORA:A_LORA]

    z = -(w0_ref[...] + _bdot(jnp.tanh(w_lo), w2_ref[...]))
    softplus = jnp.maximum(z, 0.0) + jnp.log(1.0 + jnp.exp(-jnp.abs(z)))
    ld = -jnp.exp(-softplus - 0.5)
    a = _sigmoid(a0_ref[...] + _bdot(a_lo, a2_ref[...]))
    g_s[...] = _bdot(_sigmoid(g_lo), g2_ref[...])

    kk = k * kk_ref[...]
    kmod = k * (1.0 + (a - 1.0) * ka_ref[...])
    rkr = r * kmod * rk_ref[...]
    for h in range(A_HEADS):
        sl = slice(h * HEAD_DIM, (h + 1) * HEAD_DIM)
        kkh = kk[:, sl]
        nrm = jnp.sqrt(jnp.sum(kkh * kkh, axis=-1, keepdims=True))
        kkn = kkh / jnp.maximum(nrm, 1e-12)
        na_s[:, sl] = -kkn
        b_s[:, sl] = kkn * a[:, sl]
        bonus_s[:, sl] = jnp.sum(rkr[:, sl], axis=-1, keepdims=True) * v[:, sl]
    r_s[...] = r
    k_s[...] = kmod
    v_s[...] = v
    ld_s[...] = ld

    ri = lax.broadcasted_iota(jnp.int32, (tb, tb), 0)
    ci = lax.broadcasted_iota(jnp.int32, (tb, tb), 1)
    same_chunk = (lax.shift_right_logical(ri, CHUNK_SHIFT)
                  == lax.shift_right_logical(ci, CHUNK_SHIFT))
    tri = jnp.where((ri >= ci) & same_chunk, 1.0, 0.0).astype(BF16)
    ld_hi = ld.astype(BF16)
    ld_lo = (ld - ld_hi.astype(F32)).astype(BF16)
    cum_s[...] = (jnp.dot(tri, ld_hi, preferred_element_type=F32)
                  + jnp.dot(tri, ld_lo, preferred_element_type=F32))

    rc = lax.broadcasted_iota(jnp.int32, (CHUNK, CHUNK), 0)
    cc = lax.broadcasted_iota(jnp.int32, (CHUNK, CHUNK), 1)
    strict = rc > cc
    incl = rc >= cc
    eye = jnp.where(rc == cc, 1.0, 0.0)
    r2 = lax.broadcasted_iota(jnp.int32, (2 * CHUNK, CHUNK), 0)
    c2 = lax.broadcasted_iota(jnp.int32, (2 * CHUNK, CHUNK), 1)
    strict_incl = ((r2 < CHUNK) & (r2 > c2)) | (r2 - CHUNK >= c2)

    heads = range(A_HEADS)
    hsl = [slice(h * HEAD_DIM, (h + 1) * HEAD_DIM) for h in heads]

    chunks_per_step = 2 if n_chunks % 2 == 0 else 1

    def phase_a(cs, carry):
        ar, bk, bke, vhs, idxs = [], [], [], [], []
        for ci in range(chunks_per_step):
            c = cs * chunks_per_step + ci
            r0 = pl.multiple_of(c * CHUNK, CHUNK)
            rows = pl.ds(r0, CHUNK)
            cum = cum_s[rows, :]
            cum_last = cum[CHUNK - 1:CHUNK, :]
            w_inv = jnp.exp(-cum)
            w_end = jnp.exp(cum_last - cum)
            at = na_s[rows, :] * jnp.exp(cum - ld_s[rows, :])
            rt = r_s[rows, :] * jnp.exp(cum)
            bh = b_s[rows, :]
            kh = k_s[rows, :]
            vh = v_s[rows, :].astype(BF16)
            bt, kt = bh * w_inv, kh * w_inv
            be, ke = bh * w_end, kh * w_end
            ar += [jnp.concatenate([at[:, s], rt[:, s]], axis=0).astype(BF16) for s in hsl]
            bk += [jnp.concatenate([bt[:, s], kt[:, s]], axis=0).astype(BF16) for s in hsl]
            bke += [jnp.concatenate([be[:, s], ke[:, s]], axis=0).astype(BF16) for s in hsl]
            vhs += [vh[:, s] for s in hsl]
            idxs += [c * A_HEADS + h for h in heads]
        probs = range(len(idxs))
        gram = [lax.dot_general(ar[i], bk[i], _NT, preferred_element_type=F32) for i in probs]
        l_ab = [jnp.where(strict, g[0:CHUNK, 0:CHUNK], 0.0) for g in gram]
        tinv = [eye + m for m in l_ab]
        mpow = [_bdot(m, m) for m in l_ab]
        for _ in range(4):
            both = [_bdot(jnp.concatenate([m, t], axis=0), m) for m, t in zip(mpow, tinv)]
            mpow = [b[0:CHUNK] for b in both]
            tinv = [t + b[CHUNK:] for t, b in zip(tinv, both)]
        tinv = [t + _bdot(t, m) for t, m in zip(tinv, mpow)]
        for i in probs:
            idx = idxs[i]
            g = gram[i]
            lm = jnp.where(strict_incl, g[:, CHUNK:], 0.0).astype(BF16)
            xy = jnp.dot(lm, vhs[i], preferred_element_type=F32)
            x0_s[idx] = xy[0:CHUNK]
            y0_s[idx] = xy[CHUNK:]
            mrb_s[idx] = jnp.where(incl, g[CHUNK:, 0:CHUNK], 0.0).astype(BF16)
            tinv_s[idx] = tinv[i].astype(BF16)
            ar_s[idx] = ar[i]
            bke_s[idx] = bke[i]
        return carry

    lax.fori_loop(0, n_chunks // chunks_per_step, phase_a, 0)

    def phase_b(c, carry):
        r0 = pl.multiple_of(c * CHUNK, CHUNK)
        rows = pl.ds(r0, CHUNK)
        wc = jnp.exp(cum_s[rows, :][CHUNK - 1:CHUNK, :])
        vh = v_s[rows, :]
        idx = [c * A_HEADS + h for h in heads]
        s0 = [state_ref[h] for h in heads]
        ars = [lax.dot_general(ar_s[idx[h]], s0[h].astype(BF16), _NT,
                               preferred_element_type=F32) for h in heads]
        u = [jnp.dot(tinv_s[idx[h]], (ars[h][0:CHUNK] + x0_s[idx[h]]).astype(BF16),
                     preferred_element_type=F32) for h in heads]
        y = [ars[h][CHUNK:] + y0_s[idx[h]]
             + jnp.dot(mrb_s[idx[h]], u[h].astype(BF16), preferred_element_type=F32)
             for h in heads]
        for h in heads:
            uv = jnp.concatenate([u[h], vh[:, hsl[h]]], axis=0).astype(BF16)
            state_ref[h] = s0[h] * wc[:, hsl[h]] + lax.dot_general(
                uv, bke_s[idx[h]], _TN, preferred_element_type=F32)
        for h in heads:
            mean = jnp.mean(y[h], axis=-1, keepdims=True)
            yc = y[h] - mean
            var = jnp.mean(yc * yc, axis=-1, keepdims=True)
            y_s[rows, hsl[h]] = yc * lax.rsqrt(var + RWKV_GN_EPS)
        return carry

    lax.fori_loop(0, n_chunks, phase_b, 0)

    out = (y_s[...] * gnw_ref[...] + gnb_ref[...] + bonus_s[...]) * g_s[...]
    o_ref[...] = out.astype(o_ref.dtype)


def _rwkv(p, nb, seq, tb, mu, w0, w2, a0, a2, g2, k_k, k_a, r_k, gn_w, gn_b):
    n = p.shape[0]
    nt = seq // tb
    row2 = lambda t: t.reshape(1, -1).astype(F32)
    vec_spec = lambda width: pl.BlockSpec((1, width), lambda b, t: (0, 0))
    full = lambda arr: pl.BlockSpec(arr.shape, lambda b, t: (0, 0))
    w2b, a2b, g2b = w2.astype(BF16), a2.astype(BF16), g2.astype(BF16)
    scr = lambda: pltpu.VMEM((tb, WIDTH), F32)
    nch = (tb // CHUNK) * A_HEADS
    return pl.pallas_call(
        _rwkv_kernel,
        grid=(nb, nt),
        in_specs=[pl.BlockSpec((tb, 3 * WIDTH), lambda b, t: (b * nt + t, 0)),
                  pl.BlockSpec((tb, A_LORA), lambda b, t: (b * nt + t, LORA_OFF // A_LORA)),
                  vec_spec(3 * WIDTH), vec_spec(A_LORA), vec_spec(WIDTH), full(w2b),
                  vec_spec(WIDTH), full(a2b), full(g2b), vec_spec(WIDTH), vec_spec(WIDTH),
                  vec_spec(WIDTH), vec_spec(WIDTH), vec_spec(WIDTH)],
        out_specs=pl.BlockSpec((tb, WIDTH), lambda b, t: (b * nt + t, 0)),
        out_shape=jax.ShapeDtypeStruct((n, WIDTH), BF16),
        scratch_shapes=[pltpu.VMEM((A_HEADS, HEAD_DIM, HEAD_DIM), F32),
                        pltpu.VMEM((1, 3 * WIDTH), F32),
                        pltpu.VMEM((1, A_LORA), F32)] + [scr() for _ in range(10)]
        + [pltpu.VMEM((nch, CHUNK, CHUNK), F32), pltpu.VMEM((nch, CHUNK, CHUNK), F32),
           pltpu.VMEM((nch, CHUNK, CHUNK), BF16), pltpu.VMEM((nch, CHUNK, CHUNK), BF16),
           pltpu.VMEM((nch, 2 * CHUNK, HEAD_DIM), BF16),
           pltpu.VMEM((nch, 2 * CHUNK, HEAD_DIM), BF16)],
        compiler_params=_params(("parallel", "arbitrary")),
        name="rwkv",
    )(p, p, row2(mu[:3 * WIDTH]), row2(mu[3 * WIDTH:]), row2(w0), w2b, row2(a0), a2b, g2b,
      row2(k_k), row2(k_a), row2(r_k), row2(gn_w), row2(gn_b))


def _band_kernel(q_ref, kp_ref, kc_ref, vp_ref, vc_ref, bias_ref, o_ref, k_s, vt_s):
    tq = q_ref.shape[0]
    i = pl.program_id(1)
    pair = 2 * CHUNK
    nkeys = B_BAND + CHUNK
    hsl = [slice(h * HEAD_DIM, (h + 1) * HEAD_DIM) for h in range(B_HEADS)]

    pad_rows = vt_s.shape[1] - HEAD_DIM
    rid = lax.broadcasted_iota(jnp.int32, (pad_rows, 2 * tq), 0)
    ones_rows = jnp.where(rid == 0, 1.0, 0.0).astype(BF16)
    for half, (kr, vr) in enumerate(((kp_ref, vp_ref), (kc_ref, vc_ref))):
        rows = slice(half * tq, (half + 1) * tq)
        vt = vr[...].astype(F32).T
        for h in range(B_HEADS):
            k_s[h, rows, :] = kr[:, hsl[h]]
            vt_s[h, 0:HEAD_DIM, rows] = vt[hsl[h], :].astype(BF16)
    for h in range(B_HEADS):
        vt_s[h, HEAD_DIM:, :] = ones_rows

    q = q_ref[...] * (HEAD_DIM ** -0.5)
    krow = lax.broadcasted_iota(jnp.int32, (nkeys, pair), 0)
    for pi in range(tq // pair):
        off = pi * pair
        valid = jnp.logical_or(i > 0, krow + off >= tq)
        heads = range(B_HEADS)
        s = [lax.dot_general(k_s[h, off:off + nkeys, :], q[off:off + pair, hsl[h]], _NT,
                             preferred_element_type=F32) for h in heads]
        s = [jnp.where(valid, s[h] + bias_ref[h], NEG_INF) for h in heads]
        m = [jnp.max(s[h], axis=0, keepdims=True) for h in heads]
        e = [jnp.exp(s[h] - m[h]).astype(BF16) for h in heads]
        acc = [jnp.dot(vt_s[h, :, off:off + nkeys], e[h], preferred_element_type=F32)
               for h in heads]
        outs = [a[0:HEAD_DIM] / a[HEAD_DIM:HEAD_DIM + 1] for a in acc]
        o_ref[off:off + pair, :] = jnp.concatenate(outs, axis=0).T.astype(o_ref.dtype)


def _band(p, nb, seq, tq, bias):
    n = p.shape[0]
    nt = seq // tq
    qb, kb, vb = B_OFF // WIDTH, B_OFF // WIDTH + 1, B_OFF // WIDTH + 2
    cur = lambda cb: pl.BlockSpec((tq, WIDTH), lambda b, t: (b * nt + t, cb))
    prv = lambda cb: pl.BlockSpec((tq, WIDTH), lambda b, t: (b * nt + jnp.maximum(t - 1, 0), cb))
    return pl.pallas_call(
        _band_kernel,
        grid=(nb, nt),
        in_specs=[cur(qb), prv(kb), cur(kb), prv(vb), cur(vb),
                  pl.BlockSpec(bias.shape, lambda b, t: (0, 0, 0))],
        out_specs=pl.BlockSpec((tq, WIDTH), lambda b, t: (b * nt + t, 0)),
        out_shape=jax.ShapeDtypeStruct((n, WIDTH), BF16),
        scratch_shapes=[pltpu.VMEM((B_HEADS, 2 * tq, HEAD_DIM), BF16),
                        pltpu.VMEM((B_HEADS, HEAD_DIM + BF16_SUBLANES, 2 * tq), BF16)],
        compiler_params=_params(("parallel", "arbitrary")),
        name="band",
    )(p, p, p, p, p, bias)


def _diff_kernel(q_ref, k_ref, v_ref, bd_ref, bp_ref,
                 lq1_ref, lk1_ref, lq2_ref, lk2_ref, sub_ref, o_ref,
                 k_s, vt_s, m_s, acc_s, *, lam_init):
    tq = q_ref.shape[0]
    seq = k_ref.shape[0]
    hw = 2 * HEAD_DIM
    i = pl.program_id(2)

    @pl.when(i == 0)
    def _():
        for comp in range(2):
            k_s[comp] = k_ref[:, comp * HEAD_DIM:(comp + 1) * HEAD_DIM]
        pad_rows = vt_s.shape[1] - hw
        rid = lax.broadcasted_iota(jnp.int32, (pad_rows, tq), 0)
        ones_row = jnp.where(rid == 0, 1.0, 0.0).astype(BF16)
        for j in range(seq // tq):
            vt = v_ref[j * tq:(j + 1) * tq, :].astype(F32).T.astype(BF16)
            vt_s[j] = jnp.concatenate([vt, ones_row], axis=0)

    q = q_ref[...] * (HEAD_DIM ** -0.5)
    qs = (q[:, 0:HEAD_DIM], q[:, HEAD_DIM:])

    def scores(j, comp, bias, width=1):
        r0 = pl.multiple_of(j * tq, tq)
        kc = k_s[comp, pl.ds(r0, width * tq), :]
        s = lax.dot_general(kc, qs[comp], _NT, preferred_element_type=F32)
        return s if bias is None else s + bias

    comps = range(2)

    def first_tile(j, bias):
        s = [scores(j, c, bias) for c in comps]
        m = [jnp.max(s[c], axis=0, keepdims=True) for c in comps]
        e = [jnp.exp(s[c] - m[c]).astype(BF16) for c in comps]
        vt = vt_s[j]
        for c in comps:
            m_s[c] = m[c]
            acc_s[c] = jnp.dot(vt, e[c], preferred_element_type=F32)

    def next_tile(j, bias, width=1):
        s = [scores(j, c, bias, width) for c in comps]
        m_old = [m_s[c] for c in comps]
        m_new = [jnp.maximum(m_old[c], jnp.max(s[c], axis=0, keepdims=True)) for c in comps]
        e = [jnp.exp(s[c] - m_new[c]).astype(BF16) for c in comps]
        alpha = [jnp.exp(m_old[c] - m_new[c]) for c in comps]
        pv = [jnp.dot(vt_s[j], e[c][0:tq], preferred_element_type=F32) for c in comps]
        for t in range(1, width):
            pv = [pv[c] + jnp.dot(vt_s[j + t], e[c][t * tq:(t + 1) * tq],
                                  preferred_element_type=F32) for c in comps]
        for c in comps:
            m_s[c] = m_new[c]
            acc_s[c] = alpha[c] * acc_s[c] + pv[c]

    first_tile(i, bd_ref[...])

    @pl.when(i >= 1)
    def _():
        next_tile(i - 1, bp_ref[...])

    n_far = jnp.maximum(i - 1, 0)

    def far_pair(jj, carry):
        next_tile(2 * jj, None, width=2)
        return carry

    lax.fori_loop(0, lax.shift_right_logical(n_far, 1), far_pair, 0)

    @pl.when(lax.bitwise_and(n_far, 1) == 1)
    def _():
        next_tile(n_far - 1, None)

    lam = (jnp.exp(jnp.sum(lq1_ref[...] * lk1_ref[...], axis=-1, keepdims=True))
           - jnp.exp(jnp.sum(lq2_ref[...] * lk2_ref[...], axis=-1, keepdims=True))
           + lam_init)
    a1, a2 = acc_s[0], acc_s[1]
    out = a1[0:hw] / a1[hw:hw + 1] - lam * (a2[0:hw] / a2[hw:hw + 1])
    out = out * lax.rsqrt(jnp.mean(out * out, axis=0, keepdims=True) + SUBLN_EPS)
    out = out * sub_ref[...] * (1.0 - lam_init)
    o_ref[...] = out.T.astype(o_ref.dtype)


def _diff(p, nb, seq, tq, bias_diag, bias_prev, lq1, lk1, lq2, lk2, subw, lam_init):
    n = p.shape[0]
    nt = seq // tq
    hw = 2 * HEAD_DIM
    vrows = hw + BF16_SUBLANES
    qb, kb, vb = C_OFF // hw, (C_OFF + WIDTH) // hw, (C_OFF + 2 * WIDTH) // hw
    row2 = lambda t: t.reshape(1, -1).astype(F32)
    vec = lambda width: pl.BlockSpec((1, width), lambda b, h, t: (0, 0))
    return pl.pallas_call(
        functools.partial(_diff_kernel, lam_init=lam_init),
        grid=(nb, C_HEADS, nt),
        in_specs=[pl.BlockSpec((tq, hw), lambda b, h, t: (b * nt + t, qb + h)),
                  pl.BlockSpec((seq, hw), lambda b, h, t: (b, kb + h)),
                  pl.BlockSpec((seq, hw), lambda b, h, t: (b, vb + h)),
                  pl.BlockSpec((None, tq, tq), lambda b, h, t: (h, 0, 0)),
                  pl.BlockSpec((None, tq, tq), lambda b, h, t: (h, 0, 0)),
                  vec(HEAD_DIM), vec(HEAD_DIM), vec(HEAD_DIM), vec(HEAD_DIM),
                  pl.BlockSpec((hw, 1), lambda b, h, t: (0, 0))],
        out_specs=pl.BlockSpec((tq, hw), lambda b, h, t: (b * nt + t, h)),
        out_shape=jax.ShapeDtypeStruct((n, WIDTH), BF16),
        scratch_shapes=[pltpu.VMEM((2, seq, HEAD_DIM), BF16),
                        pltpu.VMEM((nt, vrows, tq), BF16),
                        pltpu.VMEM((2, 1, tq), F32),
                        pltpu.VMEM((2, vrows, tq), F32)],
        compiler_params=_params(("parallel", "parallel", "arbitrary")),
        name="diffattn",
    )(p, p, p, bias_diag, bias_prev, row2(lq1), row2(lk1), row2(lq2),
      row2(lk2), subw.reshape(-1, 1).astype(F32))


def _t5_buckets(rel):
    nb = T5_BUCKETS // 2
    max_exact = nb // 2
    ret = (rel > 0).astype(jnp.int32) * nb
    n = jnp.abs(rel)
    nf = jnp.maximum(n, 1).astype(jnp.float32)
    large = max_exact + (jnp.log(nf / max_exact) / math.log(T5_MAX_DIST / max_exact)
                         * (nb - max_exact)).astype(jnp.int32)
    large = jnp.minimum(large, nb - 1)
    return ret + jnp.where(n < max_exact, n, large)


def _toeplitz(vec, rows, cols):
    h, period = vec.shape
    flat = jnp.tile(vec, (1, rows))[:, :rows * (period - 1)]
    return flat.reshape(h, rows, period - 1)[:, :, :cols]


def _t5_tiles(t5_table, tq):
    tab = t5_table.astype(F32)
    period = 2 * tq
    m = jnp.arange(period)
    qk = jnp.where(m < tq, m, m - period)
    far = tab[_t5_buckets(jnp.int32(-2 * tq))]
    vec_d = tab[_t5_buckets(-qk)].T - far[:, None]
    vec_p = tab[_t5_buckets(-tq - qk)].T - far[:, None]
    diag = _toeplitz(vec_d, tq, tq)
    prev = _toeplitz(vec_p, tq, tq)
    qi = jnp.arange(tq)[None, :]
    ki = jnp.arange(tq)[:, None]
    diag = jnp.where((ki // CHUNK) <= (qi // CHUNK), diag, NEG_INF)
    return diag, prev


def _band_bias(rel_bias, keys, queries):
    period = keys + queries
    m = jnp.arange(period)
    qk = jnp.where(m < queries, m, m - period)
    dist = B_LEFT_CHUNKS * CHUNK + qk
    vec = rel_bias[:, jnp.clip(dist, -REL_CLIP, REL_CLIP) + REL_CLIP].astype(F32)
    bias = _toeplitz(vec, keys, queries)
    qi = jnp.arange(queries)[None, :]
    ki = jnp.arange(keys)[:, None]
    in_window = jnp.where(qi < CHUNK, ki < B_BAND, ki >= CHUNK)
    return jnp.where(in_window, bias, NEG_INF)


def _outproj_kernel(ya_ref, yb_ref, yc_ref, pb_ref, pc_ref, ph_ref, pcp_ref, php_ref,
                    cw_ref, w_ref, x_ref, g_ref, rwh_ref, rwl_ref, rb_ref, o_ref, route_ref,
                    *, tiles_per_seq):
    tm = x_ref.shape[0]
    i = pl.program_id(0)
    u = pc_ref[...].astype(F32) * ph_ref[...].astype(F32)
    up = pcp_ref[...].astype(F32) * php_ref[...].astype(F32)
    up = jnp.where(i % tiles_per_seq == 0, 0.0, up)
    row = lax.broadcasted_iota(jnp.int32, (tm, 1), 0)
    s1 = jnp.where(row == 0, up[7:8, :], pltpu.roll(u, 1, axis=0))
    s2 = pltpu.roll(u, 2, axis=0)
    s2 = jnp.where(row == 0, up[6:7, :], jnp.where(row == 1, up[7:8, :], s2))
    cw = cw_ref[...]
    yd = pb_ref[...].astype(F32) * (cw[0:1, :] * s2 + cw[1:2, :] * s1 + cw[2:3, :] * u)
    acc = jnp.dot(ya_ref[...], w_ref[0:WIDTH, :], preferred_element_type=F32)
    acc += jnp.dot(yb_ref[...], w_ref[WIDTH:2 * WIDTH, :], preferred_element_type=F32)
    acc += jnp.dot(yc_ref[...], w_ref[2 * WIDTH:3 * WIDTH, :], preferred_element_type=F32)
    acc += jnp.dot(yd.astype(BF16), w_ref[3 * WIDTH:, :], preferred_element_type=F32)
    x_new = x_ref[...] + acc
    o_ref[...] = x_new
    route_ref[...] = _route(x_new, g_ref[...], rwh_ref[...], rwl_ref[...], rb_ref[...])


def _outproj(ya, yb, yc, p, conv_w, w_bf16, x2d, gain_ffn, wr_hi, wr_lo, br, seq, tm):
    n, d = x2d.shape
    db = D_OFF // WIDTH
    r8 = tm // 8
    ycur = pl.BlockSpec((tm, WIDTH), lambda i: (i, 0))
    pcur = lambda cb: pl.BlockSpec((tm, WIDTH), lambda i: (i, cb))
    pprev = lambda cb: pl.BlockSpec((8, WIDTH), lambda i: (jnp.maximum(i * r8 - 1, 0), cb))
    return pl.pallas_call(
        functools.partial(_outproj_kernel, tiles_per_seq=seq // tm),
        grid=(n // tm,),
        in_specs=[ycur, ycur, ycur, pcur(db), pcur(db + 1), pcur(db + 2),
                  pprev(db + 1), pprev(db + 2),
                  pl.BlockSpec(conv_w.shape, lambda i: (0, 0)),
                  pl.BlockSpec(w_bf16.shape, lambda i: (0, 0)),
                  pl.BlockSpec((tm, d), lambda i: (i, 0)),
                  pl.BlockSpec((1, d), lambda i: (0, 0)),
                  pl.BlockSpec((d, LANES), lambda i: (0, 0)),
                  pl.BlockSpec((d, LANES), lambda i: (0, 0)),
                  pl.BlockSpec((1, LANES), lambda i: (0, 0))],
        out_specs=[pl.BlockSpec((tm, d), lambda i: (i, 0)),
                   pl.BlockSpec((tm, LANES), lambda i: (i, 0))],
        out_shape=[jax.ShapeDtypeStruct((n, d), F32),
                   jax.ShapeDtypeStruct((n, LANES), F32)],
        compiler_params=_params(("parallel",)),
        name="outproj",
    )(ya, yb, yc, p, p, p, p, p, conv_w.astype(F32), w_bf16, x2d, gain_ffn, wr_hi, wr_lo, br)


def _route(x, gain, w_hi, w_lo, bias):
    ms = jnp.mean(x * x, axis=-1, keepdims=True)
    xn = x * lax.rsqrt(ms + RMS_EPS) * gain
    xh = xn.astype(BF16)
    xl = (xn - xh.astype(F32)).astype(BF16)
    logits = (jnp.dot(xh, w_hi, preferred_element_type=F32)
              + jnp.dot(xh, w_lo, preferred_element_type=F32)
              + jnp.dot(xl, w_hi, preferred_element_type=F32)) + bias
    tm = x.shape[0]
    lane_i = lax.broadcasted_iota(jnp.int32, (tm, LANES), 1)
    lane = lane_i.astype(F32)
    lane_grp = lax.shift_right_logical(lane_i, 3).astype(F32)
    is_g = (lane_i >= N_EXPERTS) & (lane_i < N_EXPERTS + N_GROUPS)
    gl = jnp.where(is_g, logits, NEG_INF)
    gmax = jnp.max(gl, axis=-1, keepdims=True)
    gsum = jnp.sum(jnp.where(is_g, jnp.exp(gl - gmax), 0.0), axis=-1, keepdims=True)
    g_w = 1.0 / gsum
    g_idx = jnp.min(jnp.where(is_g & (gl == gmax), lane, float(LANES)), axis=-1,
                    keepdims=True) - float(N_EXPERTS)
    in_grp = (lane_i < N_EXPERTS) & (lane_grp == g_idx)
    el = jnp.where(in_grp, logits, NEG_INF)
    emax = jnp.max(el, axis=-1, keepdims=True)
    ee = jnp.where(in_grp, jnp.exp(el - emax), 0.0)
    ep = ee / jnp.sum(ee, axis=-1, keepdims=True)
    p1 = jnp.max(ep, axis=-1, keepdims=True)
    i1 = jnp.min(jnp.where(in_grp & (ep == p1), lane, float(LANES)), axis=-1, keepdims=True)
    rest = in_grp & (lane != i1)
    ep2 = jnp.where(rest, ep, -1.0)
    p2 = jnp.max(ep2, axis=-1, keepdims=True)
    i2 = jnp.min(jnp.where(rest & (ep2 == p2), lane, float(LANES)), axis=-1, keepdims=True)
    tot = p1 + p2
    return (jnp.where(lane_i == ROUTE_ID1, i1, 0.0)
            + jnp.where(lane_i == ROUTE_ID2, i2, 0.0)
            + jnp.where(lane_i == ROUTE_W1, g_w * p1 / tot, 0.0)
            + jnp.where(lane_i == ROUTE_W2, g_w * p2 / tot, 0.0))


def _gather_pipeline(step, n_steps, n_live, idx_hbm, src_hbm, idx_smem, buf, isem, gsem):
    groups = buf.shape[1]

    def idx_copy(j, slot):
        return pltpu.make_async_copy(idx_hbm.at[pl.ds(j * IDX_TILE, IDX_TILE)],
                                     idx_smem.at[pl.ds(slot * IDX_TILE, IDX_TILE)],
                                     isem.at[slot])

    def row_copy(token, g, sub, slot):
        src = src_hbm.at[lax.shift_right_logical(token, 3),
                         pl.ds(lax.bitwise_and(token, F32_SUBLANES - 1), 1)]
        return pltpu.make_async_copy(src, buf.at[slot, g, pl.ds(sub, 1)], gsem.at[slot])

    def start_rows(slot):
        def body(g, carry):
            base = slot * IDX_TILE + g * F32_SUBLANES
            for sub in range(F32_SUBLANES):
                row_copy(idx_smem[base + sub], g, sub, slot).start(priority=sub % 2)
            return carry
        lax.fori_loop(0, groups, body, 0)

    def wait_rows(slot):
        def body(g, carry):
            for sub in range(F32_SUBLANES):
                row_copy(0, g, sub, slot).wait()
            return carry
        lax.fori_loop(0, groups, body, 0)

    cur = lax.rem(step, 2)
    nxt = 1 - cur

    @pl.when(step == 0)
    def _():
        first = idx_copy(0, 0)
        first.start()
        first.wait()
        start_rows(0)
        if n_steps > 1:
            idx_copy(1, 1).start()

    @pl.when(step + 1 < n_steps)
    def _():
        idx_copy(step + 1, nxt).wait()

        @pl.when(step + 1 < n_live)
        def _():
            start_rows(nxt)

    @pl.when(step + 2 < n_steps)
    def _():
        idx_copy(step + 2, cur).start()

    @pl.when(step < n_live)
    def _():
        wait_rows(cur)

    return cur


def _dispatch_kernel(lastg_ref, hast_ref, nv_ref, idx_hbm, x_ref, xs_hbm,
                     idx_smem, zbuf, isem, ssem, zsem, *, n_steps):
    j = pl.program_id(0)
    groups = x_ref.shape[0]
    tm = groups * F32_SUBLANES
    tile_groups = zbuf.shape[0]
    n_tiles = xs_hbm.shape[0] // tile_groups
    cur = lax.rem(j, 2)

    def idx_copy(step, slot):
        return pltpu.make_async_copy(idx_hbm.at[pl.ds(step * IDX_TILE, IDX_TILE)],
                                     idx_smem.at[pl.ds(slot * IDX_TILE, IDX_TILE)],
                                     isem.at[slot])

    def zero_copy(group):
        return pltpu.make_async_copy(zbuf, xs_hbm.at[pl.ds(group, tile_groups)], zsem.at[0])

    @pl.when(j == 0)
    def _():
        idx_copy(0, 0).start()
        zbuf[...] = jnp.zeros_like(zbuf)
        n_live = nv_ref[0]
        for e in range(N_EXPERTS):
            @pl.when(hast_ref[e] > 0)
            def _():
                zero_copy(lastg_ref[e]).start()

            @pl.when(n_live + e < n_tiles)
            def _():
                zero_copy((n_live + e) * tile_groups).start()
        for e in range(N_EXPERTS):
            @pl.when(hast_ref[e] > 0)
            def _():
                zero_copy(0).wait()

            @pl.when(n_live + e < n_tiles)
            def _():
                zero_copy(0).wait()

    idx_copy(j, cur).wait()

    @pl.when(j + 1 < n_steps)
    def _():
        idx_copy(j + 1, 1 - cur).start()

    def row_copy(g, sub, dest):
        dst = xs_hbm.at[lax.shift_right_logical(dest, 3),
                        pl.ds(lax.bitwise_and(dest, F32_SUBLANES - 1), 1)]
        return pltpu.make_async_copy(x_ref.at[g, pl.ds(sub, 1)], dst, ssem.at[0])

    def start_rows(g, carry):
        base = cur * IDX_TILE + g * F32_SUBLANES
        for sub in range(F32_SUBLANES):
            row_copy(g, sub, idx_smem[base + sub]).start(priority=0)
            row_copy(g, sub, idx_smem[base + tm + sub]).start(priority=1)
        return carry

    def wait_rows(g, carry):
        for sub in range(2 * F32_SUBLANES):
            row_copy(g, sub % F32_SUBLANES, 0).wait()
        return carry

    lax.fori_loop(0, groups, start_rows, 0)
    lax.fori_loop(0, groups, wait_rows, 0)


def _dispatch(x2d, pos_idx, last_group, has_tile, n_live, n_rows, tm, tm_e):
    n, d = x2d.shape
    n_steps = n // tm
    g = F32_SUBLANES
    return pl.pallas_call(
        functools.partial(_dispatch_kernel, n_steps=n_steps),
        grid_spec=pltpu.PrefetchScalarGridSpec(
            num_scalar_prefetch=3,
            grid=(n_steps,),
            in_specs=[pl.BlockSpec(memory_space=pl.ANY),
                      pl.BlockSpec((tm // g, g, d), lambda j, lg, ht, nv: (j, 0, 0))],
            out_specs=pl.BlockSpec(memory_space=pl.ANY),
            scratch_shapes=[pltpu.SMEM((2 * IDX_TILE,), jnp.int32),
                            pltpu.VMEM((tm_e // g, g, d), F32),
                            pltpu.SemaphoreType.DMA((2,)),
                            pltpu.SemaphoreType.DMA((1,)),
                            pltpu.SemaphoreType.DMA((1,))]),
        out_shape=jax.ShapeDtypeStruct((n_rows // g, g, d), F32),
        compiler_params=_params(("arbitrary",)),
        name="dispatch",
    )(last_group, has_tile, n_live, pos_idx, x2d.reshape(n // g, g, d))


def _experts_kernel(te_ref, nv_ref, x_ref, g_ref, wg_ref, wu_ref, wd_ref, y_ref,
                    wg_s, wu_s, wd_s):
    j = pl.program_id(0)
    n_live = nv_ref[0]

    @pl.when(jnp.logical_or(j == 0, te_ref[j] != te_ref[jnp.maximum(j - 1, 0)]))
    def _():
        wg_s[...] = wg_ref[...].astype(BF16)
        wu_s[...] = wu_ref[...].astype(BF16)
        wd_s[...] = wd_ref[...].astype(BF16)

    @pl.when(j < n_live)
    def _():
        x = x_ref[...]
        ms = jnp.mean(x * x, axis=-1, keepdims=True)
        xn = (x * lax.rsqrt(ms + RMS_EPS) * g_ref[...]).astype(BF16)
        hg = jnp.dot(xn, wg_s[...], preferred_element_type=F32)
        hu = jnp.dot(xn, wu_s[...], preferred_element_type=F32)
        h = hg * _sigmoid(hg) * hu
        y_ref[...] = jnp.dot(h.astype(BF16), wd_s[...], preferred_element_type=F32)

    @pl.when(j >= n_live)
    def _():
        y_ref[...] = jnp.zeros_like(y_ref)


def _experts(xs, gain, tile_expert, n_live, wg, wu, wd, layer, tm):
    n_rows, d = xs.shape
    hid = wg.shape[-1]
    n_steps = tile_expert.shape[0]
    live = lambda j, nv: jnp.minimum(j, nv[0] - 1)
    return pl.pallas_call(
        _experts_kernel,
        grid_spec=pltpu.PrefetchScalarGridSpec(
            num_scalar_prefetch=2,
            grid=(n_steps,),
            in_specs=[pl.BlockSpec((tm, d), lambda j, te, nv: (live(j, nv), 0)),
                      pl.BlockSpec((1, d), lambda j, te, nv: (0, 0)),
                      pl.BlockSpec((None, None, d, hid), lambda j, te, nv: (layer, te[j], 0, 0)),
                      pl.BlockSpec((None, None, d, hid), lambda j, te, nv: (layer, te[j], 0, 0)),
                      pl.BlockSpec((None, None, hid, d), lambda j, te, nv: (layer, te[j], 0, 0))],
            out_specs=pl.BlockSpec((tm, d), lambda j, te, nv: (j, 0)),
            scratch_shapes=[pltpu.VMEM((d, hid), BF16), pltpu.VMEM((d, hid), BF16),
                            pltpu.VMEM((hid, d), BF16)]),
        out_shape=jax.ShapeDtypeStruct((n_rows, d), F32),
        compiler_params=_params(("arbitrary",)),
        name="experts",
    )(tile_expert, n_live, xs, gain, wg, wu, wd)


def _combine_kernel(idx_hbm, y_hbm, x_ref, route_ref, gf_ref, o_ref,
                    idx_smem, ybuf, isem, gsem, *, n_steps, final_norm):
    j = pl.program_id(0)
    tm = x_ref.shape[0]
    cur = _gather_pipeline(j, n_steps, n_steps, idx_hbm, y_hbm, idx_smem, ybuf, isem, gsem)
    route = route_ref[...]
    w1 = route[:, ROUTE_W1:ROUTE_W1 + 1]
    w2 = route[:, ROUTE_W2:ROUTE_W2 + 1]
    yb = ybuf[cur].reshape(2 * tm, x_ref.shape[1])
    y = x_ref[...] + w1 * yb[0:tm] + w2 * yb[tm:]
    if final_norm:
        ms = jnp.mean(y * y, axis=-1, keepdims=True)
        y = y * lax.rsqrt(ms + RMS_EPS) * gf_ref[...]
    o_ref[...] = y


def _combine(x2d, y_rows, pos_idx, route, gain_final, final_norm, tm):
    n, d = x2d.shape
    n_steps = n // tm
    return pl.pallas_call(
        functools.partial(_combine_kernel, n_steps=n_steps, final_norm=final_norm),
        grid=(n_steps,),
        in_specs=[pl.BlockSpec(memory_space=pl.ANY),
                  pl.BlockSpec(memory_space=pl.ANY),
                  pl.BlockSpec((tm, d), lambda j: (j, 0)),
                  pl.BlockSpec((tm, LANES), lambda j: (j, 0)),
                  pl.BlockSpec((1, d), lambda j: (0, 0))],
        out_specs=pl.BlockSpec((tm, d), lambda j: (j, 0)),
        out_shape=jax.ShapeDtypeStruct((n, d), F32),
        scratch_shapes=[pltpu.SMEM((2 * IDX_TILE,), jnp.int32),
                        pltpu.VMEM((2, 2 * tm // F32_SUBLANES, F32_SUBLANES, d), F32),
                        pltpu.SemaphoreType.DMA((2,)),
                        pltpu.SemaphoreType.DMA((2,))],
        compiler_params=_params(("arbitrary",)),
        name="combine",
    )(pos_idx, y_rows.reshape(-1, F32_SUBLANES, d), x2d, route, gain_final)


def _dispatch_plan(e1, e2, tm_e, tm_c):
    n = e1.shape[0]
    n_tiles = (2 * n) // tm_e + N_EXPERTS
    e = jnp.concatenate([e1, e2])
    onehot = (e[:, None] == jnp.arange(N_EXPERTS, dtype=jnp.int32)[None, :]).astype(jnp.int32)
    csum = jnp.cumsum(onehot, axis=0)
    rank = jnp.sum((csum - onehot) * onehot, axis=1)
    counts = csum[-1]
    padded = ((counts + tm_e - 1) // tm_e) * tm_e
    ends = jnp.cumsum(padded)
    dest = jnp.sum((ends - padded)[None, :] * onehot, axis=1) + rank
    pos = jnp.concatenate([dest[:n].reshape(n // tm_c, tm_c), dest[n:].reshape(n // tm_c, tm_c)],
                          axis=1)
    pos = jnp.pad(pos, ((0, 0), (0, IDX_TILE - 2 * tm_c))).reshape(-1)
    tile_start = jnp.arange(n_tiles, dtype=jnp.int32) * tm_e
    tile_expert = jnp.minimum(jnp.sum((tile_start[:, None] >= ends[None, :]).astype(jnp.int32),
                                      axis=1), N_EXPERTS - 1)
    n_live = (ends[-1] // tm_e).reshape(1)
    last_group = jnp.maximum(ends - tm_e, 0) // F32_SUBLANES
    has_tile = (padded > 0).astype(jnp.int32)
    return (pos, tile_expert.astype(jnp.int32), n_live.astype(jnp.int32),
            last_group.astype(jnp.int32), has_tile)


def _pick(total, pref):
    t = min(pref, total)
    while total % t:
        t //= 2
    return t


def kernel(x, w_in, w_out, norm_mix, norm_ffn, norm_final, rwkv_mu, rwkv_w0, rwkv_w2, rwkv_a0, rwkv_a2, rwkv_g2, rwkv_k_k, rwkv_k_a, rwkv_r_k, rwkv_gn_w, rwkv_gn_b, band_rel_bias, t5_rel_bias, diff_lambda_q1, diff_lambda_k1, diff_lambda_q2, diff_lambda_k2, diff_subln_w, conv_w, router_group_w, router_group_b, router_expert_w, router_expert_b, expert_w_gate, expert_w_up, expert_w_down):
    nb, seq, d = x.shape
    n = nb * seq
    depth = w_in.shape[0]
    x2d = x.reshape(n, d)

    tm_proj = _pick(n, 1024)
    tm_moe = _pick(n, 512)
    tm_exp = _pick(n, 512)
    t_rwkv = _pick(seq, 256)
    t_band = _pick(seq, 512)
    t_diff = _pick(seq, 512)
    row2 = lambda t: t.reshape(1, -1).astype(F32)

    t5_diag, t5_prev = _t5_tiles(t5_rel_bias, t_diff)

    for l in range(depth):
        wl = w_in[l]
        w_perm = jnp.concatenate([wl[:, :3 * WIDTH], wl[:, A_COLS:], wl[:, 3 * WIDTH:A_COLS]],
                                 axis=1).astype(BF16)
        p = _inproj(x2d, row2(norm_mix[l]), w_perm, tm_proj, 1280)

        ya = _rwkv(p, nb, seq, t_rwkv, rwkv_mu[l], rwkv_w0[l], rwkv_w2[l], rwkv_a0[l],
                   rwkv_a2[l], rwkv_g2[l], rwkv_k_k[l], rwkv_k_a[l], rwkv_r_k[l],
                   rwkv_gn_w[l], rwkv_gn_b[l])
        yb = _band(p, nb, seq, t_band,
                   _band_bias(band_rel_bias[l], B_BAND + CHUNK, 2 * CHUNK))
        lam_init = 0.8 - 0.6 * math.exp(-0.3 * l)
        yc = _diff(p, nb, seq, t_diff, t5_diag, t5_prev,
                   diff_lambda_q1[l], diff_lambda_k1[l], diff_lambda_q2[l],
                   diff_lambda_k2[l], diff_subln_w[l], lam_init)
        wr = jnp.concatenate([router_expert_w[l], router_group_w[l]], axis=1).astype(F32)
        wr = jnp.pad(wr, ((0, 0), (0, LANES - wr.shape[1])))
        wr_hi = wr.astype(BF16)
        wr_lo = (wr - wr_hi.astype(F32)).astype(BF16)
        br = jnp.concatenate([router_expert_b[l], router_group_b[l]]).astype(F32)
        br = jnp.pad(br, (0, LANES - br.shape[0])).reshape(1, LANES)
        x2d, route = _outproj(ya, yb, yc, p, conv_w[l], w_out[l].astype(BF16), x2d,
                              row2(norm_ffn[l]), wr_hi, wr_lo, br, seq, tm_moe)
        pos_idx, tile_expert, n_live, last_group, has_tile = _dispatch_plan(
            route[:, ROUTE_ID1].astype(jnp.int32), route[:, ROUTE_ID2].astype(jnp.int32),
            tm_exp, tm_moe)
        n_rows = tile_expert.shape[0] * tm_exp
        xs = _dispatch(x2d, pos_idx, last_group, has_tile, n_live, n_rows, tm_moe, tm_exp)
        y_rows = _experts(xs.reshape(n_rows, d), row2(norm_ffn[l]), tile_expert, n_live,
                          expert_w_gate, expert_w_up, expert_w_down, l, tm_exp)
        x2d = _combine(x2d, y_rows, pos_idx, route, row2(norm_final), l == depth - 1, tm_moe)
    return x2d.reshape(nb, seq, d)
```
